```python
import jax, jax.numpy as jnp
from jax import lax
import numpy as np

D_MODEL = 1024
BATCH = 8
SEQ = 4096
DEPTH = 1

GRID_W = 64
N_MEM = 256
EPS = 1e-6
Q_BLOCK = 128

NA_HEADS = 8
NA_HEAD_DIM = 64
NA_WIN_R = 8
NA_WIN_C = 16
MLA_HEADS = 8
MLA_NOPE = 64
MLA_ROPE = 32
MLA_V = 64
MLA_Q_RANK = 384
MLA_KV_RANK = 256
ROPE_BASE = 10000.0
MEM_HEADS = 4
MEM_HEAD_DIM = 128
N_EXPERTS = 16
EC_FACTOR = 2
EXPERT_FF = 1024

NA_W = NA_HEADS * NA_HEAD_DIM
MLA_W = MLA_HEADS * MLA_V
MEM_W = MEM_HEADS * MEM_HEAD_DIM
N_BRANCH = 3
IN_SIZES = [NA_W, NA_W, NA_W, MLA_Q_RANK, MLA_KV_RANK, MLA_ROPE, MEM_W, N_BRANCH * D_MODEL]
IN_COLS = int(sum(IN_SIZES))
IN_SPLITS = [int(v) for v in np.cumsum(IN_SIZES)[:-1]]

kernel_name = 'hybrid_na_mla_mem_ec_block'


def rmsnorm(x, g):
    xf = x.astype(jnp.float32)
    y = xf * lax.rsqrt(jnp.mean(xf * xf, axis=-1, keepdims=True) + EPS)
    return (y * g.astype(jnp.float32)).astype(x.dtype)


def rope_1d(x, pos):
    d = x.shape[-1]
    inv = ROPE_BASE ** (-jnp.arange(0, d, 2, dtype=jnp.float32) / d)
    ang = pos.astype(jnp.float32)[:, None] * inv[None, :]
    cos = jnp.cos(ang)[None, :, None, :]
    sin = jnp.sin(ang)[None, :, None, :]
    xf = x.astype(jnp.float32)
    x1, x2 = xf[..., : d // 2], xf[..., d // 2:]
    return jnp.concatenate([x1 * cos - x2 * sin, x1 * sin + x2 * cos], axis=-1).astype(x.dtype)


def axial_rope(x, row, col):
    half = x.shape[-1] // 2
    return jnp.concatenate([rope_1d(x[..., :half], row), rope_1d(x[..., half:], col)], axis=-1)


def neighbourhood_attention(q, k, v, rpb):
    b, s, h, dh = q.shape
    rows = s // GRID_W
    kr = min(NA_WIN_R, rows)
    kc = NA_WIN_C
    q = q.reshape(b, rows, GRID_W, h, dh)
    k = k.reshape(b, rows, GRID_W, h, dh)
    v = v.reshape(b, rows, GRID_W, h, dh)
    cols = jnp.arange(GRID_W)
    col_start = jnp.clip(cols - kc // 2, 0, GRID_W - kc)
    col_idx = col_start[:, None] + jnp.arange(kc)[None, :]
    bias_c = col_idx - cols[:, None] + (NA_WIN_C - 1)
    rpb = rpb.astype(jnp.float32)
    scale = dh ** -0.5

    def row_block(r):
        rs = jnp.clip(r - kr // 2, 0, rows - kr)
        k_band = lax.dynamic_slice_in_dim(k, rs, kr, axis=1)
        v_band = lax.dynamic_slice_in_dim(v, rs, kr, axis=1)
        k_win = k_band[:, :, col_idx]
        v_win = v_band[:, :, col_idx]
        q_row = lax.dynamic_index_in_dim(q, r, axis=1, keepdims=False)
        logits = jnp.einsum('bqhd,brqkhd->bhqrk', q_row, k_win,
                            preferred_element_type=jnp.float32) * scale
        dr = rs + jnp.arange(kr) - r + (NA_WIN_R - 1)
        bias = rpb[:, dr][:, :, bias_c]
        logits = logits + bias.transpose(0, 2, 1, 3)[None]
        p = jax.nn.softmax(logits.reshape(b, h, GRID_W, kr * kc), axis=-1)
        p = p.reshape(b, h, GRID_W, kr, kc).astype(v.dtype)
        return jnp.einsum('bhqrk,brqkhd->bqhd', p, v_win)

    out = lax.map(row_block, jnp.arange(rows))
    return out.transpose(1, 0, 2, 3, 4).reshape(b, s, h * dh)


def blocked_attention(q, k, v, scale):
    b, s, h, d = q.shape
    dv = v.shape[-1]
    nb = s // Q_BLOCK
    qb = q.reshape(b, nb, Q_BLOCK, h, d).transpose(1, 0, 2, 3, 4)

    def one(qblk):
        logits = jnp.einsum('bqhd,bkhd->bhqk', qblk, k,
                            preferred_element_type=jnp.float32) * scale
        p = jax.nn.softmax(logits, axis=-1).astype(v.dtype)
        return jnp.einsum('bhqk,bkhd->bqhd', p, v)

    out = lax.map(one, qb)
    return out.transpose(1, 0, 2, 3, 4).reshape(b, s, h * dv)


def latent_attention(c_q, c_kv, k_rope, q_norm, kv_norm, w_uq, w_ukv, row, col):
    b, s, _ = c_q.shape
    q = (rmsnorm(c_q, q_norm) @ w_uq).reshape(b, s, MLA_HEADS, MLA_NOPE + MLA_ROPE)
    q_nope, q_pe = q[..., :MLA_NOPE], q[..., MLA_NOPE:]
    q = jnp.concatenate([q_nope, axial_rope(q_pe, row, col)], axis=-1)
    kv = (rmsnorm(c_kv, kv_norm) @ w_ukv).reshape(b, s, MLA_HEADS, MLA_NOPE + MLA_V)
    k_nope, v = kv[..., :MLA_NOPE], kv[..., MLA_NOPE:]
    k_pe = axial_rope(k_rope[:, :, None, :], row, col)
    k = jnp.concatenate([k_nope, jnp.broadcast_to(k_pe, (b, s, MLA_HEADS, MLA_ROPE))], axis=-1)
    return blocked_attention(q, k, v, (MLA_NOPE + MLA_ROPE) ** -0.5)


def memory_attention(q, mem, g_mem, w_mem_kv):
    b, s, _ = q.shape
    m = mem.shape[1]
    kv = rmsnorm(mem, g_mem) @ w_mem_kv
    k = kv[..., :MEM_W].reshape(b, m, MEM_HEADS, MEM_HEAD_DIM)
    v = kv[..., MEM_W:].reshape(b, m, MEM_HEADS, MEM_HEAD_DIM)
    q = q.reshape(b, s, MEM_HEADS, MEM_HEAD_DIM)
    logits = jnp.einsum('bshd,bmhd->bhsm', q, k,
                        preferred_element_type=jnp.float32) * (MEM_HEAD_DIM ** -0.5)
    p = jax.nn.softmax(logits, axis=-1).astype(v.dtype)
    return jnp.einsum('bhsm,bmhd->bshd', p, v).reshape(b, s, MEM_W)


def expert_choice_ffn(x, w_router, b_router, w_gate, w_up, w_down):
    b, s, d = x.shape
    cap = EC_FACTOR * s // N_EXPERTS
    logits = jnp.einsum('bsd,de->bse', x, w_router,
                        preferred_element_type=jnp.float32) + b_router.astype(jnp.float32)
    aff = jax.nn.softmax(logits, axis=-1)
    g, idx = lax.top_k(aff.transpose(0, 2, 1), cap)
    xs = jax.vmap(lambda xb, ib: xb[ib])(x, idx)
    hg = jnp.einsum('becd,edf->becf', xs, w_gate)
    hu = jnp.einsum('becd,edf->becf', xs, w_up)
    y = jnp.einsum('becf,efd->becd', jax.nn.silu(hg) * hu, w_down)
    y = y * g[..., None].astype(y.dtype)
    return jax.vmap(lambda yb, ib: jnp.zeros((s, d), yb.dtype).at[ib.reshape(-1)].add(yb.reshape(-1, d)))(y, idx)


def setup_inputs(seed: int = 0) -> dict:
    key = jax.random.key(seed)
    ks = jax.random.split(key, 24)
    L = DEPTH
    nrm = lambda k, shape: jax.random.normal(k, shape, jnp.float32)
    w = lambda k, shape, fan_in: nrm(k, shape) * (fan_in ** -0.5)
    gain = lambda k, shape: 1.0 + 0.05 * nrm(k, shape)
    return {
        'x': nrm(ks[0], (BATCH, SEQ, D_MODEL)),
        'mem': nrm(ks[1], (BATCH, N_MEM, D_MODEL)),
        'g_mix': gain(ks[2], (L, D_MODEL)),
        'w_in': w(ks[3], (L, D_MODEL, IN_COLS), D_MODEL),
        'na_rpb': 0.1 * nrm(ks[4], (L, NA_HEADS, 2 * NA_WIN_R - 1, 2 * NA_WIN_C - 1)),
        'mla_q_norm': gain(ks[5], (L, MLA_Q_RANK)),
        'w_mla_uq': w(ks[6], (L, MLA_Q_RANK, MLA_HEADS * (MLA_NOPE + MLA_ROPE)), MLA_Q_RANK),
        'mla_kv_norm': gain(ks[7], (L, MLA_KV_RANK)),
        'w_mla_ukv': w(ks[8], (L, MLA_KV_RANK, MLA_HEADS * (MLA_NOPE + MLA_V)), MLA_KV_RANK),
        'g_mem': gain(ks[9], (L, D_MODEL)),
        'w_mem_kv': w(ks[10], (L, D_MODEL, 2 * MEM_W), D_MODEL),
        'w_o_na': w(ks[11], (L, NA_W, D_MODEL), NA_W),
        'w_o_mla': w(ks[12], (L, MLA_W, D_MODEL), MLA_W),
        'w_o_mem': w(ks[13], (L, MEM_W, D_MODEL), MEM_W),
        'w_out': w(ks[14], (L, D_MODEL, D_MODEL), D_MODEL),
        'g_ffn': gain(ks[15], (L, D_MODEL)),
        'w_router': w(ks[16], (L, D_MODEL, N_EXPERTS), D_MODEL),
        'b_router': 0.01 * nrm(ks[17], (L, N_EXPERTS)),
        'w_gate': w(ks[18], (L, N_EXPERTS, D_MODEL, EXPERT_FF), D_MODEL),
        'w_up': w(ks[19], (L, N_EXPERTS, D_MODEL, EXPERT_FF), D_MODEL),
        'w_down': w(ks[20], (L, N_EXPERTS, EXPERT_FF, D_MODEL), EXPERT_FF),
        'g_final': gain(ks[21], (D_MODEL,)),
    }


def reference(x, mem, g_mix, w_in, na_rpb, mla_q_norm, w_mla_uq, mla_kv_norm, w_mla_ukv,
              g_mem, w_mem_kv, w_o_na, w_o_mla, w_o_mem, w_out, g_ffn, w_router, b_router,
              w_gate, w_up, w_down, g_final):
    b, s, d = x.shape
    t = jnp.arange(s)
    row, col = t // GRID_W, t % GRID_W
    h = x
    for l in range(DEPTH):
        n = rmsnorm(h, g_mix[l])
        z = n @ w_in[l]
        q_na, k_na, v_na, c_q, c_kv, k_rope, q_mem, gate_logits = jnp.split(z, IN_SPLITS, axis=-1)
        shp = (b, s, NA_HEADS, NA_HEAD_DIM)
        y_na = neighbourhood_attention(q_na.reshape(shp), k_na.reshape(shp), v_na.reshape(shp), na_rpb[l])
        y_mla = latent_attention(c_q, c_kv, k_rope, mla_q_norm[l], mla_kv_norm[l],
                                 w_mla_uq[l], w_mla_ukv[l], row, col)
        y_mem = memory_attention(q_mem, mem, g_mem[l], w_mem_kv[l])
        gates = jax.nn.sigmoid(gate_logits.reshape(b, s, N_BRANCH, d))
        merged = (gates[:, :, 0] * (y_na @ w_o_na[l])
                  + gates[:, :, 1] * (y_mla @ w_o_mla[l])
                  + gates[:, :, 2] * (y_mem @ w_o_mem[l]))
        h = h + merged @ w_out[l]
        h = h + expert_choice_ffn(rmsnorm(h, g_ffn[l]), w_router[l], b_router[l],
                                  w_gate[l], w_up[l], w_down[l])
    return rmsnorm(h, g_final)
```

```python
import functools

import jax
import jax.numpy as jnp
import numpy as np
from jax import lax
from jax.experimental import pallas as pl
from jax.experimental.pallas import tpu as pltpu

GRID_W = 64
EPS = 1e-6
NA_HEADS = 8
NA_HEAD_DIM = 64
NA_WIN_R = 8
NA_WIN_C = 16
MLA_HEADS = 8
MLA_NOPE = 64
MLA_ROPE = 32
MLA_V = 64
ROPE_BASE = 10000.0
MEM_HEADS = 4
MEM_HEAD_DIM = 128
N_EXPERTS = 16
EC_FACTOR = 2
NA_W = NA_HEADS * NA_HEAD_DIM
MEM_W = MEM_HEADS * MEM_HEAD_DIM
LANES = 128
NEG = -1e30
VMEM_LIMIT = 56 * 1024 * 1024

F32 = jnp.float32
BF16 = jnp.bfloat16


def _dot(a, b):
    return jnp.dot(a, b, preferred_element_type=F32)


def _dot_nt(a, b):
    return lax.dot_general(a, b, (((1,), (1,)), ((), ())), preferred_element_type=F32)


def _rms(xf, g):
    return xf * lax.rsqrt(jnp.mean(xf * xf, axis=-1, keepdims=True) + EPS) * g


def _sigmoid(v):
    return 1.0 / (1.0 + jnp.exp(-v))


def _full(shape):
    return pl.BlockSpec(shape, lambda *_: (0,) * len(shape))


def _params(n_axes):
    return pltpu.CompilerParams(dimension_semantics=("arbitrary",) * n_axes,
                                vmem_limit_bytes=VMEM_LIMIT)


def _inproj_kernel(x_ref, gmix_ref, wna_ref, wcq_ref, wckv_ref, wkr_ref, wqm_ref, qn_ref, kvn_ref,
                   wuq_ref, wukv_ref, cos_ref, sin_ref,
                   na_ref, qmla_ref, kmla_ref, vmla_ref, qmem_ref, *, mla_scale, mem_scale):
    n = _rms(x_ref[...], gmix_ref[...]).astype(BF16)
    z_na = _dot(n, wna_ref[...])
    na_ref[:, :NA_W] = (z_na[:, :NA_W] * (NA_HEAD_DIM ** -0.5)).astype(BF16)
    na_ref[:, NA_W:] = z_na[:, NA_W:].astype(BF16)
    qmem_ref[...] = (_dot(n, wqm_ref[...]) * mem_scale).astype(BF16)

    cos = cos_ref[...]
    sin = sin_ref[...]
    zkr = _dot(n, wkr_ref[...])
    kpe = zkr[:, :LANES] * cos + zkr[:, LANES:] * sin

    cqn = _rms(_dot(n, wcq_ref[...]), qn_ref[...]).astype(BF16)
    zq = _dot(cqn, wuq_ref[...])
    hw = MLA_HEADS * LANES
    for h in range(MLA_HEADS):
        a = zq[:, h * LANES:(h + 1) * LANES]
        b = zq[:, hw + h * LANES:hw + (h + 1) * LANES]
        qmla_ref[:, h * LANES:(h + 1) * LANES] = ((a * cos + b * sin) * mla_scale).astype(BF16)

    ckvn = _rms(_dot(n, wckv_ref[...]), kvn_ref[...]).astype(BF16)
    zkv = _dot(ckvn, wukv_ref[...])
    for h in range(MLA_HEADS):
        kmla_ref[:, h * LANES:(h + 1) * LANES] = (zkv[:, h * LANES:(h + 1) * LANES] + kpe).astype(BF16)
    vmla_ref[...] = zkv[:, hw:].astype(BF16)


def _inproj(x2, gmix, wna, wcq, wckv, wkr, wqm, qn, kvn, wuq, wukv, cos, sin, *, seq, tm):
    t, d = x2.shape
    nblk = seq // tm
    row = lambda w: pl.BlockSpec((tm, w), lambda i: (i, 0))
    rope = pl.BlockSpec((tm, LANES), lambda i: (i % nblk, 0))
    hw = MLA_HEADS * LANES
    kern = functools.partial(_inproj_kernel, mla_scale=(MLA_NOPE + MLA_ROPE) ** -0.5,
                             mem_scale=MEM_HEAD_DIM ** -0.5)
    return pl.pallas_call(
        kern,
        grid=(t // tm,),
        in_specs=[row(d), _full(gmix.shape), _full(wna.shape), _full(wcq.shape), _full(wckv.shape),
                  _full(wkr.shape), _full(wqm.shape), _full(qn.shape), _full(kvn.shape),
                  _full(wuq.shape), _full(wukv.shape), rope, rope],
        out_specs=[row(3 * NA_W), row(hw), row(hw), row(MLA_HEADS * MLA_V), row(MEM_W)],
        out_shape=[jax.ShapeDtypeStruct((t, 3 * NA_W), BF16),
                   jax.ShapeDtypeStruct((t, hw), BF16),
                   jax.ShapeDtypeStruct((t, hw), BF16),
                   jax.ShapeDtypeStruct((t, MLA_HEADS * MLA_V), BF16),
                   jax.ShapeDtypeStruct((t, MEM_W), BF16)],
        compiler_params=_params(1),
        name="inproj",
    )(x2, gmix, wna, wcq, wckv, wkr, wqm, qn, kvn, wuq, wukv, cos, sin)


def _na_kernel(q_ref, k_ref, v_ref, bias_ref, y_ref, *, rows):
    r = pl.program_id(1)
    rs = jnp.clip(r - NA_WIN_R // 2, 0, rows - NA_WIN_R)
    start = pl.multiple_of(rs * GRID_W, GRID_W)
    band = NA_WIN_R * GRID_W
    lane = lax.broadcasted_iota(jnp.int32, (GRID_W, LANES), 1)
    first = lane < NA_HEAD_DIM
    for hp in range(NA_HEADS // 2):
        cols = slice(hp * LANES, (hp + 1) * LANES)
        qp = q_ref[:, cols]
        kp = k_ref[pl.ds(start, band), cols]
        vp = v_ref[pl.ds(start, band), cols]
        outs = []
        for s in range(2):
            qm = jnp.where(first if s == 0 else ~first, qp, jnp.zeros_like(qp))
            lg = _dot_nt(qm, kp) + bias_ref[0, 2 * hp + s]
            p = jnp.exp(lg - jnp.max(lg, axis=1, keepdims=True))
            l = jnp.sum(p, axis=1, keepdims=True)
            outs.append(_dot(p.astype(BF16), vp) * (1.0 / l))
        y_ref[:, cols] = jnp.where(first, outs[0], outs[1]).astype(BF16)


def _na_bias_table(rpb):
    dd = jnp.arange(NA_WIN_R)[:, None, None, None]
    i = jnp.arange(NA_WIN_R)[None, :, None, None]
    c = jnp.arange(GRID_W)[None, None, :, None]
    j = jnp.arange(GRID_W)[None, None, None, :]
    cstart = jnp.clip(c - NA_WIN_C // 2, 0, GRID_W - NA_WIN_C)
    valid = (j >= cstart) & (j < cstart + NA_WIN_C)
    dr = i - dd + (NA_WIN_R - 1)
    bc = jnp.clip(j - c + (NA_WIN_C - 1), 0, 2 * NA_WIN_C - 2)
    tab = rpb.astype(F32)[:, dr, bc]
    tab = jnp.where(valid[None], tab, NEG)
    tab = tab.transpose(1, 0, 3, 2, 4)
    return tab.reshape(NA_WIN_R, NA_HEADS, GRID_W, NA_WIN_R * GRID_W)


def _na_attention(na, bias, *, batch, seq):
    rows = seq // GRID_W
    assert rows >= NA_WIN_R
    band = NA_WIN_R * GRID_W

    def bias_map(b, r):
        return (r - jnp.clip(r - NA_WIN_R // 2, 0, rows - NA_WIN_R), 0, 0, 0)

    return pl.pallas_call(
        functools.partial(_na_kernel, rows=rows),
        grid=(batch, rows),
        in_specs=[pl.BlockSpec((GRID_W, NA_W), lambda b, r: (b * rows + r, 0)),
                  pl.BlockSpec((seq, NA_W), lambda b, r: (b, 1)),
                  pl.BlockSpec((seq, NA_W), lambda b, r: (b, 2)),
                  pl.BlockSpec((1, NA_HEADS, GRID_W, band), bias_map)],
        out_specs=pl.BlockSpec((GRID_W, NA_W), lambda b, r: (b * rows + r, 0)),
        out_shape=jax.ShapeDtypeStruct((batch * seq, NA_W), BF16),
        compiler_params=_params(2),
        name="na_attn",
    )(na, na, na, bias)


def _mla_kernel(q_ref, k_ref, v_ref, o_ref):
    tq = q_ref.shape[0]
    first = lax.broadcasted_iota(jnp.int32, (tq, LANES), 1) < MLA_V
    v = v_ref[...]
    outs = []
    for s in range(2):
        cols = slice(s * LANES, (s + 1) * LANES)
        lg = _dot_nt(q_ref[:, cols], k_ref[:, cols])
        p = jnp.exp(lg - jnp.max(lg, axis=1, keepdims=True))
        l = jnp.sum(p, axis=1, keepdims=True)
        outs.append(_dot(p.astype(BF16), v) * (1.0 / l))
    o_ref[...] = jnp.where(first, outs[0], outs[1]).astype(BF16)


def _mla_attention(q, k, v, *, batch, seq, tq):
    nq = seq // tq
    hp = MLA_HEADS // 2
    return pl.pallas_call(
        _mla_kernel,
        grid=(batch, hp, nq),
        in_specs=[pl.BlockSpec((tq, 2 * LANES), lambda b, h, i: (b * nq + i, h)),
                  pl.BlockSpec((seq, 2 * LANES), lambda b, h, i: (b, h)),
                  pl.BlockSpec((seq, 2 * MLA_V), lambda b, h, i: (b, h))],
        out_specs=pl.BlockSpec((tq, 2 * MLA_V), lambda b, h, i: (b * nq + i, h)),
        out_shape=jax.ShapeDtypeStruct((batch * seq, MLA_HEADS * MLA_V), BF16),
        compiler_params=_params(3),
        name="mla_attn",
    )(q, k, v)


def _memkv_kernel(mem_ref, g_ref, w_ref, kv_ref):
    kv_ref[...] = _dot(_rms(mem_ref[...], g_ref[...]).astype(BF16), w_ref[...]).astype(BF16)


def _memkv(mem2, g, w, *, batch, n_mem):
    d = mem2.shape[1]
    return pl.pallas_call(
        _memkv_kernel,
        grid=(batch,),
        in_specs=[pl.BlockSpec((n_mem, d), lambda b: (b, 0)), _full(g.shape), _full(w.shape)],
        out_specs=pl.BlockSpec((n_mem, 2 * MEM_W), lambda b: (b, 0)),
        out_shape=jax.ShapeDtypeStruct((batch * n_mem, 2 * MEM_W), BF16),
        compiler_params=_params(1),
        name="memkv",
    )(mem2, g, w)


def _merge_kernel(x_ref, yna_ref, ymla_ref, qmem_ref, kvm_ref, gmix_ref, wg_ref, wona_ref, womla_ref,
                  womem_ref, wout_ref, gffn_ref, wr_ref, br_ref, h1_ref, xn_ref, lg_ref):
    x = x_ref[...]
    d = x.shape[1]
    n = _rms(x, gmix_ref[...]).astype(BF16)

    ymem = []
    for h in range(MEM_HEADS):
        cols = slice(h * LANES, (h + 1) * LANES)
        lg = _dot_nt(qmem_ref[:, cols], kvm_ref[:, cols])
        p = jnp.exp(lg - jnp.max(lg, axis=1, keepdims=True))
        l = jnp.sum(p, axis=1, keepdims=True)
        vh = kvm_ref[:, MEM_W + h * LANES:MEM_W + (h + 1) * LANES]
        ymem.append((_dot(p.astype(BF16), vh) * (1.0 / l)).astype(BF16))
    ymem = jnp.concatenate(ymem, axis=1)

    merged = None
    for j, (y, w_ref) in enumerate(((yna_ref[...], wona_ref), (ymla_ref[...], womla_ref), (ymem, womem_ref))):
        gate = _sigmoid(_dot(n, wg_ref[:, j * d:(j + 1) * d]))
        term = gate * _dot(y, w_ref[...])
        merged = term if merged is None else merged + term
    h1 = x + _dot(merged.astype(BF16), wout_ref[...])
    h1_ref[...] = h1

    xn = _rms(h1, gffn_ref[...])
    lg_ref[...] = jnp.dot(xn, wr_ref[...], preferred_element_type=F32,
                          precision=lax.Precision.HIGHEST) + br_ref[...]
    xn_ref[...] = xn


def _merge(x2, yna, ymla, qmem, kvm, gmix, wg, wona, womla, womem, wout, gffn, wr, br, *, seq, n_mem, tm):
    t, d = x2.shape
    nblk = seq // tm
    row = lambda w: pl.BlockSpec((tm, w), lambda i: (i, 0))
    return pl.pallas_call(
        _merge_kernel,
        grid=(t // tm,),
        in_specs=[row(d), row(NA_W), row(MLA_HEADS * MLA_V), row(MEM_W),
                  pl.BlockSpec((n_mem, 2 * MEM_W), lambda i: (i // nblk, 0)),
                  _full(gmix.shape), _full(wg.shape), _full(wona.shape), _full(womla.shape),
                  _full(womem.shape), _full(wout.shape), _full(gffn.shape), _full(wr.shape), _full(br.shape)],
        out_specs=[row(d), row(d), row(LANES)],
        out_shape=[jax.ShapeDtypeStruct((t, d), F32),
                   jax.ShapeDtypeStruct((t, d), F32),
                   jax.ShapeDtypeStruct((t, LANES), F32)],
        compiler_params=_params(1),
        name="merge",
    )(x2, yna, ymla, qmem, kvm, gmix, wg, wona, womla, womem, wout, gffn, wr, br)


def _route_kernel(lg_ref, idx_ref, aff_ref, bnd_ref, cs_ref, *, cap, nsplit, tb):
    s = lg_ref.shape[0]
    lg = lg_ref[...]
    ex = jnp.exp(lg - jnp.max(lg, axis=1, keepdims=True))
    aff = ex / jnp.sum(ex, axis=1, keepdims=True)
    aff_ref[...] = aff

    def search(i, lo):
        cand = lo | lax.shift_left(jnp.int32(1), 30 - i)
        cnt = jnp.sum((aff >= lax.bitcast_convert_type(cand, F32)).astype(jnp.int32), axis=0, keepdims=True)
        return jnp.where(cnt >= cap, cand, lo)

    thr = lax.bitcast_convert_type(lax.fori_loop(0, 31, search, jnp.zeros((1, LANES), jnp.int32)), F32)
    gt = aff > thr
    eq = aff == thr
    need = cap - jnp.sum(gt.astype(jnp.int32), axis=0, keepdims=True)

    tri = (lax.broadcasted_iota(jnp.int32, (tb, tb), 0)
           >= lax.broadcasted_iota(jnp.int32, (tb, tb), 1)).astype(BF16)

    def cumsum_blocks(mask_of_block):
        carry = jnp.zeros((1, LANES), F32)
        for blk in range(s // tb):
            rows = slice(blk * tb, (blk + 1) * tb)
            c = _dot(tri, mask_of_block(rows).astype(BF16)) + carry
            cs_ref[rows, :] = c
            carry = c[tb - 1:tb, :]

    cumsum_blocks(lambda rows: eq[rows, :])
    eq_rank = cs_ref[...] - eq.astype(F32)
    sel = gt | (eq & (eq_rank < need.astype(F32)))
    cumsum_blocks(lambda rows: sel[rows, :])

    sq = s // nsplit
    sub = lax.broadcasted_iota(jnp.int32, (8, LANES), 0)
    bnd = jnp.zeros((8, LANES), jnp.int32)
    for k in range(1, nsplit + 1):
        bnd = jnp.where(sub == k, cs_ref[k * sq - 1:k * sq, :].astype(jnp.int32), bnd)
    bnd_ref[0] = bnd

    jrow = lax.broadcasted_iota(jnp.int32, (1, cap), 1).astype(F32)
    ones = jnp.ones((8, tb), BF16)
    for e in range(N_EXPERTS):
        cnt = jnp.zeros((8, cap), F32)
        for blk in range(s // tb):
            col = cs_ref[blk * tb:(blk + 1) * tb, e:e + 1]
            cnt = cnt + _dot(ones, jnp.where(col <= jrow, 1.0, 0.0).astype(BF16))
        idx_ref[0, e:e + 1, :] = cnt[0:1, :].astype(jnp.int32)


def _route(lg, *, batch, seq, cap, nsplit):
    tb = min(512, seq)
    return pl.pallas_call(
        functools.partial(_route_kernel, cap=cap, nsplit=nsplit, tb=tb),
        grid=(batch,),
        in_specs=[pl.BlockSpec((seq, LANES), lambda b: (b, 0))],
        out_specs=[pl.BlockSpec((1, N_EXPERTS, cap), lambda b: (b, 0, 0)),
                   pl.BlockSpec((seq, LANES), lambda b: (b, 0)),
                   pl.BlockSpec((1, 8, LANES), lambda b: (b, 0, 0))],
        out_shape=[jax.ShapeDtypeStruct((batch, N_EXPERTS, cap), jnp.int32),
                   jax.ShapeDtypeStruct((batch * seq, LANES), F32),
                   jax.ShapeDtypeStruct((batch, 8, LANES), jnp.int32)],
        scratch_shapes=[pltpu.VMEM((seq, LANES), F32)],
        compiler_params=_params(1),
        name="route",
    )(lg)


def _expert_kernel(idx_ref, xn_ref, aff_ref, wg_ref, wu_ref, wd_ref, yg_ref, xs_ref, ga_ref, *, cap):
    e = pl.program_id(1)

    def gather(i, c):
        t = idx_ref[0, 0, i]
        xs_ref[pl.ds(i, 1), :] = xn_ref[pl.ds(t, 1), :]
        ga_ref[pl.ds(i, 1), :] = aff_ref[pl.ds(t, 1), :]
        return c

    lax.fori_loop(0, cap, gather, 0, unroll=8)

    xs = xs_ref[...].astype(BF16)
    hg = _dot(xs, wg_ref[0])
    hu = _dot(xs, wu_ref[0])
    act = (hg * _sigmoid(hg) * hu).astype(BF16)
    y = _dot(act, wd_ref[0])
    lane = lax.broadcasted_iota(jnp.int32, ga_ref.shape, 1)
    g = jnp.sum(jnp.where(lane == e, ga_ref[...], 0.0), axis=1, keepdims=True)
    yg_ref[0] = y * g


def _experts(idx3, xn, aff, wg, wu, wd, *, batch, seq, cap):
    d = wg.shape[1]
    ff = wg.shape[2]
    ne = N_EXPERTS
    once = pl.Buffered(1)
    return pl.pallas_call(
        functools.partial(_expert_kernel, cap=cap),
        grid=(batch, ne),
        in_specs=[pl.BlockSpec((1, 1, cap), lambda b, e: (b * ne + e, 0, 0), memory_space=pltpu.SMEM),
                  pl.BlockSpec((seq, d), lambda b, e: (b, 0), pipeline_mode=once),
                  pl.BlockSpec((seq, LANES), lambda b, e: (b, 0), pipeline_mode=once),
                  pl.BlockSpec((1, d, ff), lambda b, e: (e, 0, 0)),
                  pl.BlockSpec((1, d, ff), lambda b, e: (e, 0, 0)),
                  pl.BlockSpec((1, ff, d), lambda b, e: (e, 0, 0))],
        out_specs=pl.BlockSpec((1, cap, d), lambda b, e: (b * ne + e, 0, 0)),
        out_shape=jax.ShapeDtypeStruct((batch * ne, cap, d), F32),
        scratch_shapes=[pltpu.VMEM((cap, d), F32), pltpu.VMEM((cap, LANES), F32)],
        compiler_params=_params(2),
        name="experts",
    )(idx3, xn, aff, wg, wu, wd)


def _combine_kernel(idx_ref, bnd_ref, h1_ref, yg_ref, gfin_ref, out_ref, *, sq):
    q = pl.program_id(1)
    e = pl.program_id(2)

    @pl.when(e == 0)
    def _():
        out_ref[...] = h1_ref[...]

    base = q * sq

    def scatter(i, c):
        t = idx_ref[0, 0, i] - base
        out_ref[pl.ds(t, 1), :] += yg_ref[0, pl.ds(i, 1), :]
        return c

    lax.fori_loop(bnd_ref[0, q, e], bnd_ref[0, q + 1, e], scatter, 0)

    @pl.when(e == pl.num_programs(2) - 1)
    def _():
        out_ref[...] = _rms(out_ref[...], gfin_ref[...])


def _combine(idx3, bnd, h1, yg, gfin, *, batch, seq, cap, nsplit):
    d = h1.shape[1]
    ne = N_EXPERTS
    sq = seq // nsplit
    return pl.pallas_call(
        functools.partial(_combine_kernel, sq=sq),
        grid=(batch, nsplit, ne),
        in_specs=[pl.BlockSpec((1, 1, cap), lambda b, q, e: (b * ne + e, 0, 0), memory_space=pltpu.SMEM),
                  pl.BlockSpec((1, 8, LANES), lambda b, q, e: (b, 0, 0), memory_space=pltpu.SMEM),
                  pl.BlockSpec((sq, d), lambda b, q, e: (b * nsplit + q, 0)),
                  pl.BlockSpec((1, cap, d), lambda b, q, e: (b * ne + e, 0, 0)),
                  _full(gfin.shape)],
        out_specs=pl.BlockSpec((sq, d), lambda b, q, e: (b * nsplit + q, 0)),
        out_shape=jax.ShapeDtypeStruct((batch * seq, d), F32),
        compiler_params=_params(3),
        name="combine",
    )(idx3, bnd, h1, yg, gfin)


def _rope_tables(seq):
    t = jnp.arange(seq)
    half = MLA_ROPE // 2
    inv = ROPE_BASE ** (-jnp.arange(0, half, 2, dtype=F32) / half)
    ang_r = (t // GRID_W).astype(F32)[:, None] * inv[None, :]
    ang_c = (t % GRID_W).astype(F32)[:, None] * inv[None, :]
    cos = jnp.concatenate([jnp.cos(ang_r), jnp.cos(ang_r), jnp.cos(ang_c), jnp.cos(ang_c)], axis=1)
    sin = jnp.concatenate([-jnp.sin(ang_r), jnp.sin(ang_r), -jnp.sin(ang_c), jnp.sin(ang_c)], axis=1)
    pad = LANES - MLA_NOPE - MLA_ROPE
    cos = jnp.concatenate([jnp.ones((seq, MLA_NOPE), F32), cos, jnp.zeros((seq, pad), F32)], axis=1)
    sin = jnp.concatenate([jnp.zeros((seq, MLA_NOPE), F32), sin, jnp.zeros((seq, pad), F32)], axis=1)
    return cos, sin


_q = MLA_ROPE // 4
ROPE_SWAP = np.concatenate([np.arange(_q, 2 * _q), np.arange(0, _q),
                            np.arange(3 * _q, 4 * _q), np.arange(2 * _q, 3 * _q)])


def kernel(x, mem, g_mix, w_in, na_rpb, mla_q_norm, w_mla_uq, mla_kv_norm, w_mla_ukv, g_mem, w_mem_kv,
           w_o_na, w_o_mla, w_o_mem, w_out, g_ffn, w_router, b_router, w_gate, w_up, w_down, g_final):
    batch, seq, d = x.shape
    n_mem = mem.shape[1]
    depth = g_mix.shape[0]
    q_rank = mla_q_norm.shape[1]
    kv_rank = mla_kv_norm.shape[1]
    cap = EC_FACTOR * seq // N_EXPERTS
    nsplit = 4
    tm = 512
    pad = LANES - MLA_NOPE - MLA_ROPE
    cos, sin = _rope_tables(seq)

    assert depth == 1
    h = x.reshape(batch * seq, d)
    for l in range(depth):
        w = w_in[l]
        o = np.cumsum([0, NA_W, NA_W, NA_W, q_rank, kv_rank, MLA_ROPE, MEM_W, 3 * d])
        wna = w[:, o[0]:o[3]].astype(BF16)
        wcq = w[:, o[3]:o[4]].astype(BF16)
        wckv = w[:, o[4]:o[5]].astype(BF16)
        wkr_raw = w[:, o[5]:o[6]]
        wqm = w[:, o[6]:o[7]].astype(BF16)
        wgate_cols = w[:, o[7]:o[8]].astype(BF16)
        z = lambda *s: jnp.zeros(s, F32)
        wkr = jnp.concatenate([z(d, MLA_NOPE), wkr_raw, z(d, pad),
                               z(d, MLA_NOPE), wkr_raw[:, ROPE_SWAP], z(d, pad)], axis=1).astype(BF16)
        uq = w_mla_uq[l].reshape(q_rank, MLA_HEADS, MLA_NOPE + MLA_ROPE)
        uq_a = jnp.concatenate([uq, z(q_rank, MLA_HEADS, pad)], axis=2)
        uq_b = jnp.concatenate([z(q_rank, MLA_HEADS, MLA_NOPE), uq[:, :, MLA_NOPE:][:, :, ROPE_SWAP],
                                z(q_rank, MLA_HEADS, pad)], axis=2)
        wuq = jnp.concatenate([uq_a.reshape(q_rank, -1), uq_b.reshape(q_rank, -1)], axis=1).astype(BF16)
        ukv = w_mla_ukv[l].reshape(kv_rank, MLA_HEADS, MLA_NOPE + MLA_V)
        uk = jnp.concatenate([ukv[:, :, :MLA_NOPE], z(kv_rank, MLA_HEADS, LANES - MLA_NOPE)], axis=2)
        wukv = jnp.concatenate([uk.reshape(kv_rank, -1), ukv[:, :, MLA_NOPE:].reshape(kv_rank, -1)],
                               axis=1).astype(BF16)
        wr = jnp.concatenate([w_router[l].astype(F32), z(d, LANES - N_EXPERTS)], axis=1)
        br = jnp.concatenate([b_router[l].astype(F32), jnp.full((LANES - N_EXPERTS,), NEG, F32)])[None, :]

        na, qmla, kmla, vmla, qmem = _inproj(
            h, g_mix[l][None, :], wna, wcq, wckv, wkr, wqm, mla_q_norm[l][None, :], mla_kv_norm[l][None, :],
            wuq, wukv, cos, sin, seq=seq, tm=tm)
        y_na = _na_attention(na, _na_bias_table(na_rpb[l]), batch=batch, seq=seq)
        y_mla = _mla_attention(qmla, kmla, vmla, batch=batch, seq=seq, tq=256)
        kvm = _memkv(mem.reshape(batch * n_mem, d), g_mem[l][None, :], w_mem_kv[l].astype(BF16),
                     batch=batch, n_mem=n_mem)
        h1, xn, lg = _merge(h, y_na, y_mla, qmem, kvm, g_mix[l][None, :], wgate_cols,
                             w_o_na[l].astype(BF16), w_o_mla[l].astype(BF16), w_o_mem[l].astype(BF16),
                             w_out[l].astype(BF16), g_ffn[l][None, :], wr, br, seq=seq, n_mem=n_mem, tm=256)
        idx, aff, bnd = _route(lg, batch=batch, seq=seq, cap=cap, nsplit=nsplit)
        idx3 = idx.reshape(batch * N_EXPERTS, 1, cap)
        yg = _experts(idx3, xn, aff, w_gate[l].astype(BF16), w_up[l].astype(BF16), w_down[l].astype(BF16),
                      batch=batch, seq=seq, cap=cap)
        h = _combine(idx3, bnd, h1, yg, g_final[None, :], batch=batch, seq=seq, cap=cap, nsplit=nsplit)
    return h.reshape(batch, seq, d)
```

```python
import functools

import jax
import jax.numpy as jnp
import numpy as np
from jax import lax
from jax.experimental import pallas as pl
from jax.experimental.pallas import tpu as pltpu

GRID_W = 64
EPS = 1e-6
NA_HEADS = 8
NA_HEAD_DIM = 64
NA_WIN_R = 8
NA_WIN_C = 16
MLA_HEADS = 8
MLA_NOPE = 64
MLA_ROPE = 32
MLA_V = 64
ROPE_BASE = 10000.0
MEM_HEADS = 4
MEM_HEAD_DIM = 128
N_EXPERTS = 16
EC_FACTOR = 2
NA_W = NA_HEADS * NA_HEAD_DIM
MEM_W = MEM_HEADS * MEM_HEAD_DIM
LANES = 128
NEG = -1e30
VMEM_LIMIT = 56 * 1024 * 1024

F32 = jnp.float32
BF16 = jnp.bfloat16


def _dot(a, b):
    return jnp.dot(a, b, preferred_element_type=F32)


def _dot_nt(a, b):
    return lax.dot_general(a, b, (((1,), (1,)), ((), ())), preferred_element_type=F32)


def _rms(xf, g):
    return xf * lax.rsqrt(jnp.mean(xf * xf, axis=-1, keepdims=True) + EPS) * g


def _sigmoid(v):
    return 1.0 / (1.0 + jnp.exp(-v))


def _full(shape):
    return pl.BlockSpec(shape, lambda *_: (0,) * len(shape))


def _params(n_axes):
    return pltpu.CompilerParams(dimension_semantics=("arbitrary",) * n_axes,
                                vmem_limit_bytes=VMEM_LIMIT)


def _inproj_kernel(x_ref, gmix_ref, wna_ref, wcq_ref, wckv_ref, wkr_ref, wqm_ref, qn_ref, kvn_ref,
                   wuq_ref, wukv_ref, cos_ref, sin_ref,
                   na_ref, qmla_ref, kmla_ref, vmla_ref, qmem_ref, *, mla_scale, mem_scale):
    n = _rms(x_ref[...], gmix_ref[...]).astype(BF16)
    z_na = _dot(n, wna_ref[...])
    na_ref[:, :NA_W] = (z_na[:, :NA_W] * (NA_HEAD_DIM ** -0.5)).astype(BF16)
    na_ref[:, NA_W:] = z_na[:, NA_W:].astype(BF16)
    qmem_ref[...] = (_dot(n, wqm_ref[...]) * mem_scale).astype(BF16)

    cos = cos_ref[...]
    sin = sin_ref[...]
    zkr = _dot(n, wkr_ref[...])
    kpe = zkr[:, :LANES] * cos + zkr[:, LANES:] * sin

    cqn = _rms(_dot(n, wcq_ref[...]), qn_ref[...]).astype(BF16)
    zq = _dot(cqn, wuq_ref[...])
    hw = MLA_HEADS * LANES
    for h in range(MLA_HEADS):
        a = zq[:, h * LANES:(h + 1) * LANES]
        b = zq[:, hw + h * LANES:hw + (h + 1) * LANES]
        qmla_ref[:, h * LANES:(h + 1) * LANES] = ((a * cos + b * sin) * mla_scale).astype(BF16)

    ckvn = _rms(_dot(n, wckv_ref[...]), kvn_ref[...]).astype(BF16)
    zkv = _dot(ckvn, wukv_ref[...])
    for h in range(MLA_HEADS):
        kmla_ref[:, h * LANES:(h + 1) * LANES] = (zkv[:, h * LANES:(h + 1) * LANES] + kpe).astype(BF16)
    vmla_ref[...] = zkv[:, hw:].astype(BF16)


def _inproj(x2, gmix, wna, wcq, wckv, wkr, wqm, qn, kvn, wuq, wukv, cos, sin, *, seq, tm):
    t, d = x2.shape
    nblk = seq // tm
    row = lambda w: pl.BlockSpec((tm, w), lambda i: (i, 0))
    rope = pl.BlockSpec((tm, LANES), lambda i: (i % nblk, 0))
    hw = MLA_HEADS * LANES
    kern = functools.partial(_inproj_kernel, mla_scale=(MLA_NOPE + MLA_ROPE) ** -0.5,
                             mem_scale=MEM_HEAD_DIM ** -0.5)
    return pl.pallas_call(
        kern,
        grid=(t // tm,),
        in_specs=[row(d), _full(gmix.shape), _full(wna.shape), _full(wcq.shape), _full(wckv.shape),
                  _full(wkr.shape), _full(wqm.shape), _full(qn.shape), _full(kvn.shape),
                  _full(wuq.shape), _full(wukv.shape), rope, rope],
        out_specs=[row(3 * NA_W), row(hw), row(hw), row(MLA_HEADS * MLA_V), row(MEM_W)],
        out_shape=[jax.ShapeDtypeStruct((t, 3 * NA_W), BF16),
                   jax.ShapeDtypeStruct((t, hw), BF16),
                   jax.ShapeDtypeStruct((t, hw), BF16),
                   jax.ShapeDtypeStruct((t, MLA_HEADS * MLA_V), BF16),
                   jax.ShapeDtypeStruct((t, MEM_W), BF16)],
        compiler_params=_params(1),
        name="inproj",
    )(x2, gmix, wna, wcq, wckv, wkr, wqm, qn, kvn, wuq, wukv, cos, sin)


NA_QROWS = NA_WIN_R // 2
NA_BAND = 12


def _na_kernel(q_ref, k_ref, v_ref, bias_ref, y_ref, *, rows):
    i = pl.program_id(1)
    us = jnp.clip(i * NA_QROWS - NA_WIN_R // 2, 0, rows - NA_BAND)
    start = pl.multiple_of(us * GRID_W, GRID_W)
    band = NA_BAND * GRID_W
    lane = lax.broadcasted_iota(jnp.int32, (NA_QROWS * GRID_W, LANES), 1)
    first = lane < NA_HEAD_DIM
    for hp in range(NA_HEADS // 2):
        cols = slice(hp * LANES, (hp + 1) * LANES)
        qp = q_ref[:, cols]
        kp = k_ref[pl.ds(start, band), cols]
        vp = v_ref[pl.ds(start, band), cols]
        outs = []
        for s in range(2):
            qm = jnp.where(first if s == 0 else ~first, qp, jnp.zeros_like(qp))
            lg = _dot_nt(qm, kp) + bias_ref[0, 2 * hp + s]
            p = jnp.exp(lg - jnp.max(lg, axis=1, keepdims=True))
            l = jnp.sum(p, axis=1, keepdims=True)
            outs.append(_dot(p.astype(BF16), vp) * (1.0 / l))
        y_ref[:, cols] = jnp.where(first, outs[0], outs[1]).astype(BF16)


def _na_bias_table(rpb):
    q, bn, wr, wc = NA_QROWS, NA_BAND, NA_WIN_R, NA_WIN_C
    rho = np.arange(q)
    own = np.stack([0 * rho, rho, 0 * rho + (bn - wr)])
    dd = np.stack([rho, 0 * rho + wr // 2, rho + wr // 2])
    ip = np.arange(bn)[None, None, :] - own[:, :, None]
    rvalid = (ip >= 0) & (ip < wr)
    dr = ip - dd[:, :, None] + (wr - 1)
    rsel = (dr[..., None] == np.arange(2 * wr - 1)) & rvalid[..., None]
    c = np.arange(GRID_W)[:, None]
    j = np.arange(GRID_W)[None, :]
    cstart = np.clip(c - wc // 2, 0, GRID_W - wc)
    cvalid = (j >= cstart) & (j < cstart + wc)
    csel = ((j - c + wc - 1)[None] == np.arange(2 * wc - 1)[:, None, None]) & cvalid[None]
    tab = jnp.einsum("hrk,vpir,kcj->vhpcij", rpb.astype(F32), jnp.asarray(rsel, F32), jnp.asarray(csel, F32),
                     precision=lax.Precision.HIGHEST)
    valid = rvalid[:, None, :, None, :, None] & cvalid[None, None, None, :, None, :]
    tab = jnp.where(jnp.asarray(valid), tab, NEG)
    return tab.reshape(3, NA_HEADS, q * GRID_W, bn * GRID_W)


def _na_attention(na, bias, *, batch, seq):
    rows = seq // GRID_W
    nblk = rows // NA_QROWS
    assert rows % NA_QROWS == 0 and rows >= NA_BAND and NA_BAND >= NA_QROWS + NA_WIN_R - 1
    nq = NA_QROWS * GRID_W

    def bias_map(b, i):
        return (jnp.where(i == 0, 0, jnp.where(i == nblk - 1, 2, 1)), 0, 0, 0)

    return pl.pallas_call(
        functools.partial(_na_kernel, rows=rows),
        grid=(batch, nblk),
        in_specs=[pl.BlockSpec((nq, NA_W), lambda b, i: (b * nblk + i, 0)),
                  pl.BlockSpec((seq, NA_W), lambda b, i: (b, 1)),
                  pl.BlockSpec((seq, NA_W), lambda b, i: (b, 2)),
                  pl.BlockSpec((1, NA_HEADS, nq, NA_BAND * GRID_W), bias_map)],
        out_specs=pl.BlockSpec((nq, NA_W), lambda b, i: (b * nblk + i, 0)),
        out_shape=jax.ShapeDtypeStruct((batch * seq, NA_W), BF16),
        compiler_params=_params(2),
        name="na_attn",
    )(na, na, na, bias)


def _mla_kernel(q_ref, k_ref, v_ref, o_ref):
    tq = q_ref.shape[0]
    first = lax.broadcasted_iota(jnp.int32, (tq, LANES), 1) < MLA_V
    v = v_ref[...]
    outs = []
    for s in range(2):
        cols = slice(s * LANES, (s + 1) * LANES)
        lg = _dot_nt(q_ref[:, cols], k_ref[:, cols])
        p = jnp.exp(lg - jnp.max(lg, axis=1, keepdims=True))
        l = jnp.sum(p, axis=1, keepdims=True)
        outs.append(_dot(p.astype(BF16), v) * (1.0 / l))
    o_ref[...] = jnp.where(first, outs[0], outs[1]).astype(BF16)


def _mla_attention(q, k, v, *, batch, seq, tq):
    nq = seq // tq
    hp = MLA_HEADS // 2
    return pl.pallas_call(
        _mla_kernel,
        grid=(batch, hp, nq),
        in_specs=[pl.BlockSpec((tq, 2 * LANES), lambda b, h, i: (b * nq + i, h)),
                  pl.BlockSpec((seq, 2 * LANES), lambda b, h, i: (b, h)),
                  pl.BlockSpec((seq, 2 * MLA_V), lambda b, h, i: (b, h))],
        out_specs=pl.BlockSpec((tq, 2 * MLA_V), lambda b, h, i: (b * nq + i, h)),
        out_shape=jax.ShapeDtypeStruct((batch * seq, MLA_HEADS * MLA_V), BF16),
        compiler_params=_params(3),
        name="mla_attn",
    )(q, k, v)


def _memkv_kernel(mem_ref, g_ref, w_ref, kv_ref):
    kv_ref[...] = _dot(_rms(mem_ref[...], g_ref[...]).astype(BF16), w_ref[...]).astype(BF16)


def _memkv(mem2, g, w, *, batch, n_mem):
    d = mem2.shape[1]
    return pl.pallas_call(
        _memkv_kernel,
        grid=(batch,),
        in_specs=[pl.BlockSpec((n_mem, d), lambda b: (b, 0)), _full(g.shape), _full(w.shape)],
        out_specs=pl.BlockSpec((n_mem, 2 * MEM_W), lambda b: (b, 0)),
        out_shape=jax.ShapeDtypeStruct((batch * n_mem, 2 * MEM_W), BF16),
        compiler_params=_params(1),
        name="memkv",
    )(mem2, g, w)


def _merge_kernel(x_ref, yna_ref, ymla_ref, qmem_ref, kvm_ref, gmix_ref, wg_ref, wona_ref, womla_ref,
                  womem_ref, wout_ref, gffn_ref, wr_ref, br_ref, h1_ref, xn_ref, lg_ref):
    x = x_ref[...]
    d = x.shape[1]
    n = _rms(x, gmix_ref[...]).astype(BF16)

    ymem = []
    for h in range(MEM_HEADS):
        cols = slice(h * LANES, (h + 1) * LANES)
        lg = _dot_nt(qmem_ref[:, cols], kvm_ref[:, cols])
        p = jnp.exp(lg - jnp.max(lg, axis=1, keepdims=True))
        l = jnp.sum(p, axis=1, keepdims=True)
        vh = kvm_ref[:, MEM_W + h * LANES:MEM_W + (h + 1) * LANES]
        ymem.append((_dot(p.astype(BF16), vh) * (1.0 / l)).astype(BF16))
    ymem = jnp.concatenate(ymem, axis=1)

    merged = None
    for j, (y, w_ref) in enumerate(((yna_ref[...], wona_ref), (ymla_ref[...], womla_ref), (ymem, womem_ref))):
        gate = _sigmoid(_dot(n, wg_ref[:, j * d:(j + 1) * d]))
        term = gate * _dot(y, w_ref[...])
        merged = term if merged is None else merged + term
    h1 = x + _dot(merged.astype(BF16), wout_ref[...])
    h1_ref[...] = h1

    xn = _rms(h1, gffn_ref[...])
    lg_ref[...] = jnp.dot(xn, wr_ref[...], preferred_element_type=F32,
                          precision=lax.Precision.HIGHEST) + br_ref[...]
    xn_ref[...] = xn


def _merge(x2, yna, ymla, qmem, kvm, gmix, wg, wona, womla, womem, wout, gffn, wr, br, *, seq, n_mem, tm):
    t, d = x2.shape
    nblk = seq // tm
    row = lambda w: pl.BlockSpec((tm, w), lambda i: (i, 0))
    return pl.pallas_call(
        _merge_kernel,
        grid=(t // tm,),
        in_specs=[row(d), row(NA_W), row(MLA_HEADS * MLA_V), row(MEM_W),
                  pl.BlockSpec((n_mem, 2 * MEM_W), lambda i: (i // nblk, 0)),
                  _full(gmix.shape), _full(wg.shape), _full(wona.shape), _full(womla.shape),
                  _full(womem.shape), _full(wout.shape), _full(gffn.shape), _full(wr.shape), _full(br.shape)],
        out_specs=[row(d), row(d), row(LANES)],
        out_shape=[jax.ShapeDtypeStruct((t, d), F32),
                   jax.ShapeDtypeStruct((t, d), F32),
                   jax.ShapeDtypeStruct((t, LANES), F32)],
        compiler_params=_params(1),
        name="merge",
    )(x2, yna, ymla, qmem, kvm, gmix, wg, wona, womla, womem, wout, gffn, wr, br)


def _route_kernel(lg_ref, idx_ref, aff_ref, bnd_ref, cs_ref, *, cap, nsplit, tb):
    s = lg_ref.shape[0]
    lg = lg_ref[...]
    ex = jnp.exp(lg - jnp.max(lg, axis=1, keepdims=True))
    aff = ex / jnp.sum(ex, axis=1, keepdims=True)
    aff_ref[...] = aff

    def search(i, lo):
        cand = lo | lax.shift_left(jnp.int32(1), 30 - i)
        cnt = jnp.sum((aff >= lax.bitcast_convert_type(cand, F32)).astype(jnp.int32), axis=0, keepdims=True)
        return jnp.where(cnt >= cap, cand, lo)

    thr = lax.bitcast_convert_type(lax.fori_loop(0, 31, search, jnp.zeros((1, LANES), jnp.int32)), F32)
    gt = aff > thr
    eq = aff == thr
    need = cap - jnp.sum(gt.astype(jnp.int32), axis=0, keepdims=True)

    tri = (lax.broadcasted_iota(jnp.int32, (tb, tb), 0)
           >= lax.broadcasted_iota(jnp.int32, (tb, tb), 1)).astype(BF16)

    def cumsum_blocks(mask_of_block):
        carry = jnp.zeros((1, LANES), F32)
        for blk in range(s // tb):
            rows = slice(blk * tb, (blk + 1) * tb)
            c = _dot(tri, mask_of_block(rows).astype(BF16)) + carry
            cs_ref[rows, :] = c
            carry = c[tb - 1:tb, :]

    cumsum_blocks(lambda rows: eq[rows, :])
    eq_rank = cs_ref[...] - eq.astype(F32)
    sel = gt | (eq & (eq_rank < need.astype(F32)))
    cumsum_blocks(lambda rows: sel[rows, :])

    sq = s // nsplit
    sub = lax.broadcasted_iota(jnp.int32, (8, LANES), 0)
    bnd = jnp.zeros((8, LANES), jnp.int32)
    for k in range(1, nsplit + 1):
        bnd = jnp.where(sub == k, cs_ref[k * sq - 1:k * sq, :].astype(jnp.int32), bnd)
    bnd_ref[0] = bnd

    jrow = lax.broadcasted_iota(jnp.int32, (1, cap), 1).astype(F32)
    ones = jnp.ones((8, tb), BF16)
    for e in range(N_EXPERTS):
        cnt = jnp.zeros((8, cap), F32)
        for blk in range(s // tb):
            col = cs_ref[blk * tb:(blk + 1) * tb, e:e + 1]
            cnt = cnt + _dot(ones, jnp.where(col <= jrow, 1.0, 0.0).astype(BF16))
        idx_ref[0, e:e + 1, :] = cnt[0:1, :].astype(jnp.int32)


def _route(lg, *, batch, seq, cap, nsplit):
    tb = min(512, seq)
    return pl.pallas_call(
        functools.partial(_route_kernel, cap=cap, nsplit=nsplit, tb=tb),
        grid=(batch,),
        in_specs=[pl.BlockSpec((seq, LANES), lambda b: (b, 0))],
        out_specs=[pl.BlockSpec((1, N_EXPERTS, cap), lambda b: (b, 0, 0)),
                   pl.BlockSpec((seq, LANES), lambda b: (b, 0)),
                   pl.BlockSpec((1, 8, LANES), lambda b: (b, 0, 0))],
        out_shape=[jax.ShapeDtypeStruct((batch, N_EXPERTS, cap), jnp.int32),
                   jax.ShapeDtypeStruct((batch * seq, LANES), F32),
                   jax.ShapeDtypeStruct((batch, 8, LANES), jnp.int32)],
        scratch_shapes=[pltpu.VMEM((seq, LANES), F32)],
        compiler_params=_params(1),
        name="route",
    )(lg)


def _expert_kernel(idx_ref, xn_ref, aff_ref, wg_ref, wu_ref, wd_ref, yg_ref, xs_ref, ga_ref, *, cap):
    e = pl.program_id(1)

    def gather(i, c):
        t = idx_ref[0, 0, i]
        xs_ref[pl.ds(i, 1), :] = xn_ref[pl.ds(t, 1), :]
        ga_ref[pl.ds(i, 1), :] = aff_ref[pl.ds(t, 1), :]
        return c

    lax.fori_loop(0, cap, gather, 0, unroll=8)

    xs = xs_ref[...].astype(BF16)
    hg = _dot(xs, wg_ref[0])
    hu = _dot(xs, wu_ref[0])
    act = (hg * _sigmoid(hg) * hu).astype(BF16)
    y = _dot(act, wd_ref[0])
    lane = lax.broadcasted_iota(jnp.int32, ga_ref.shape, 1)
    g = jnp.sum(jnp.where(lane == e, ga_ref[...], 0.0), axis=1, keepdims=True)
    yg_ref[0] = y * g


def _experts(idx3, xn, aff, wg, wu, wd, *, batch, seq, cap):
    d = wg.shape[1]
    ff = wg.shape[2]
    ne = N_EXPERTS
    once = pl.Buffered(1)
    return pl.pallas_call(
        functools.partial(_expert_kernel, cap=cap),
        grid=(batch, ne),
        in_specs=[pl.BlockSpec((1, 1, cap), lambda b, e: (b * ne + e, 0, 0), memory_space=pltpu.SMEM),
                  pl.BlockSpec((seq, d), lambda b, e: (b, 0), pipeline_mode=once),
                  pl.BlockSpec((seq, LANES), lambda b, e: (b, 0), pipeline_mode=once),
                  pl.BlockSpec((1, d, ff), lambda b, e: (e, 0, 0)),
                  pl.BlockSpec((1, d, ff), lambda b, e: (e, 0, 0)),
                  pl.BlockSpec((1, ff, d), lambda b, e: (e, 0, 0))],
        out_specs=pl.BlockSpec((1, cap, d), lambda b, e: (b * ne + e, 0, 0)),
        out_shape=jax.ShapeDtypeStruct((batch * ne, cap, d), F32),
        scratch_shapes=[pltpu.VMEM((cap, d), F32), pltpu.VMEM((cap, LANES), F32)],
        compiler_params=_params(2),
        name="experts",
    )(idx3, xn, aff, wg, wu, wd)


def _combine_kernel(idx_ref, bnd_ref, h1_ref, yg_ref, gfin_ref, out_ref, *, sq):
    q = pl.program_id(1)
    e = pl.program_id(2)

    @pl.when(e == 0)
    def _():
        out_ref[...] = h1_ref[...]

    base = q * sq

    def scatter(i, c):
        t = idx_ref[0, 0, i] - base
        out_ref[pl.ds(t, 1), :] += yg_ref[0, pl.ds(i, 1), :]
        return c

    def scatter4(k, c):
        i = lo + 4 * k
        ts = [idx_ref[0, 0, i + u] - base for u in range(4)]
        new = [out_ref[pl.ds(ts[u], 1), :] + yg_ref[0, pl.ds(i + u, 1), :] for u in range(4)]
        for u in range(4):
            out_ref[pl.ds(ts[u], 1), :] = new[u]
        return c

    lo = bnd_ref[0, q, e]
    hi = bnd_ref[0, q + 1, e]
    quads = lax.shift_right_logical(hi - lo, 2)
    lax.fori_loop(0, quads, scatter4, 0)
    lax.fori_loop(lo + 4 * quads, hi, scatter, 0)

    @pl.when(e == pl.num_programs(2) - 1)
    def _():
        out_ref[...] = _rms(out_ref[...], gfin_ref[...])


def _combine(idx3, bnd, h1, yg, gfin, *, batch, seq, cap, nsplit):
    d = h1.shape[1]
    ne = N_EXPERTS
    sq = seq // nsplit
    return pl.pallas_call(
        functools.partial(_combine_kernel, sq=sq),
        grid=(batch, nsplit, ne),
        in_specs=[pl.BlockSpec((1, 1, cap), lambda b, q, e: (b * ne + e, 0, 0), memory_space=pltpu.SMEM),
                  pl.BlockSpec((1, 8, LANES), lambda b, q, e: (b, 0, 0), memory_space=pltpu.SMEM),
                  pl.BlockSpec((sq, d), lambda b, q, e: (b * nsplit + q, 0)),
                  pl.BlockSpec((1, cap, d), lambda b, q, e: (b * ne + e, 0, 0)),
                  _full(gfin.shape)],
        out_specs=pl.BlockSpec((sq, d), lambda b, q, e: (b * nsplit + q, 0)),
        out_shape=jax.ShapeDtypeStruct((batch * seq, d), F32),
        compiler_params=_params(3),
        name="combine",
    )(idx3, bnd, h1, yg, gfin)


def _rope_tables(seq):
    t = jnp.arange(seq)
    half = MLA_ROPE // 2
    inv = ROPE_BASE ** (-jnp.arange(0, half, 2, dtype=F32) / half)
    ang_r = (t // GRID_W).astype(F32)[:, None] * inv[None, :]
    ang_c = (t % GRID_W).astype(F32)[:, None] * inv[None, :]
    cos = jnp.concatenate([jnp.cos(ang_r), jnp.cos(ang_r), jnp.cos(ang_c), jnp.cos(ang_c)], axis=1)
    sin = jnp.concatenate([-jnp.sin(ang_r), jnp.sin(ang_r), -jnp.sin(ang_c), jnp.sin(ang_c)], axis=1)
    pad = LANES - MLA_NOPE - MLA_ROPE
    cos = jnp.concatenate([jnp.ones((seq, MLA_NOPE), F32), cos, jnp.zeros((seq, pad), F32)], axis=1)
    sin = jnp.concatenate([jnp.zeros((seq, MLA_NOPE), F32), sin, jnp.zeros((seq, pad), F32)], axis=1)
    return cos, sin


_q = MLA_ROPE // 4
ROPE_SWAP = np.concatenate([np.arange(_q, 2 * _q), np.arange(0, _q),
                            np.arange(3 * _q, 4 * _q), np.arange(2 * _q, 3 * _q)])


def kernel(x, mem, g_mix, w_in, na_rpb, mla_q_norm, w_mla_uq, mla_kv_norm, w_mla_ukv, g_mem, w_mem_kv,
           w_o_na, w_o_mla, w_o_mem, w_out, g_ffn, w_router, b_router, w_gate, w_up, w_down, g_final):
    batch, seq, d = x.shape
    n_mem = mem.shape[1]
    depth = g_mix.shape[0]
    q_rank = mla_q_norm.shape[1]
    kv_rank = mla_kv_norm.shape[1]
    cap = EC_FACTOR * seq // N_EXPERTS
    nsplit = 2
    tm = 512
    pad = LANES - MLA_NOPE - MLA_ROPE
    cos, sin = _rope_tables(seq)

    assert depth == 1
    h = x.reshape(batch * seq, d)
    for l in range(depth):
        w = w_in[l]
        o = np.cumsum([0, NA_W, NA_W, NA_W, q_rank, kv_rank, MLA_ROPE, MEM_W, 3 * d])
        wna = w[:, o[0]:o[3]].astype(BF16)
        wcq = w[:, o[3]:o[4]].astype(BF16)
        wckv = w[:, o[4]:o[5]].astype(BF16)
        wkr_raw = w[:, o[5]:o[6]]
        wqm = w[:, o[6]:o[7]].astype(BF16)
        wgate_cols = w[:, o[7]:o[8]].astype(BF16)
        z = lambda *s: jnp.zeros(s, F32)
        wkr = jnp.concatenate([z(d, MLA_NOPE), wkr_raw, z(d, pad),
                               z(d, MLA_NOPE), wkr_raw[:, ROPE_SWAP], z(d, pad)], axis=1).astype(BF16)
        uq = w_mla_uq[l].reshape(q_rank, MLA_HEADS, MLA_NOPE + MLA_ROPE)
        uq_a = jnp.concatenate([uq, z(q_rank, MLA_HEADS, pad)], axis=2)
        uq_b = jnp.concatenate([z(q_rank, MLA_HEADS, MLA_NOPE), uq[:, :, MLA_NOPE:][:, :, ROPE_SWAP],
                                z(q_rank, MLA_HEADS, pad)], axis=2)
        wuq = jnp.concatenate([uq_a.reshape(q_rank, -1), uq_b.reshape(q_rank, -1)], axis=1).astype(BF16)
        ukv = w_mla_ukv[l].reshape(kv_rank, MLA_HEADS, MLA_NOPE + MLA_V)
        uk = jnp.concatenate([ukv[:, :, :MLA_NOPE], z(kv_rank, MLA_HEADS, LANES - MLA_NOPE)], axis=2)
        wukv = jnp.concatenate([uk.reshape(kv_rank, -1), ukv[:, :, MLA_NOPE:].reshape(kv_rank, -1)],
                               axis=1).astype(BF16)
        wr = jnp.concatenate([w_router[l].astype(F32), z(d, LANES - N_EXPERTS)], axis=1)
        br = jnp.concatenate([b_router[l].astype(F32), jnp.full((LANES - N_EXPERTS,), NEG, F32)])[None, :]

        na, qmla, kmla, vmla, qmem = _inproj(
            h, g_mix[l][None, :], wna, wcq, wckv, wkr, wqm, mla_q_norm[l][None, :], mla_kv_norm[l][None, :],
            wuq, wukv, cos, sin, seq=seq, tm=tm)
        y_na = _na_attention(na, _na_bias_table(na_rpb[l]), batch=batch, seq=seq)
        y_mla = _mla_attention(qmla, kmla, vmla, batch=batch, seq=seq, tq=256)
        kvm = _memkv(mem.reshape(batch * n_mem, d), g_mem[l][None, :], w_mem_kv[l].astype(BF16),
                     batch=batch, n_mem=n_mem)
        h1, xn, lg = _merge(h, y_na, y_mla, qmem, kvm, g_mix[l][None, :], wgate_cols,
                             w_o_na[l].astype(BF16), w_o_mla[l].astype(BF16), w_o_mem[l].astype(BF16),
                             w_out[l].astype(BF16), g_ffn[l][None, :], wr, br, seq=seq, n_mem=n_mem, tm=256)
        idx, aff, bnd = _route(lg, batch=batch, seq=seq, cap=cap, nsplit=nsplit)
        idx3 = idx.reshape(batch * N_EXPERTS, 1, cap)
        yg = _experts(idx3, xn, aff, w_gate[l].astype(BF16), w_up[l].astype(BF16), w_down[l].astype(BF16),
                      batch=batch, seq=seq, cap=cap)
        h = _combine(idx3, bnd, h1, yg, g_final[None, :], batch=batch, seq=seq, cap=cap, nsplit=nsplit)
    return h.reshape(batch, seq, d)
```

```python
import functools

import jax
import jax.numpy as jnp
import numpy as np
from jax import lax
from jax.experimental import pallas as pl
from jax.experimental.pallas import tpu as pltpu

GRID_W = 64
EPS = 1e-6
NA_HEADS = 8
NA_HEAD_DIM = 64
NA_WIN_R = 8
NA_WIN_C = 16
MLA_HEADS = 8
MLA_NOPE = 64
MLA_ROPE = 32
MLA_V = 64
ROPE_BASE = 10000.0
MEM_HEADS = 4
MEM_HEAD_DIM = 128
N_EXPERTS = 16
EC_FACTOR = 2
NA_W = NA_HEADS * NA_HEAD_DIM
MEM_W = MEM_HEADS * MEM_HEAD_DIM
LANES = 128
NEG = -1e30
VMEM_LIMIT = 56 * 1024 * 1024

F32 = jnp.float32
BF16 = jnp.bfloat16


def _dot(a, b):
    return jnp.dot(a, b, preferred_element_type=F32)


def _dot_nt(a, b):
    return lax.dot_general(a, b, (((1,), (1,)), ((), ())), preferred_element_type=F32)


def _rms(xf, g):
    return xf * lax.rsqrt(jnp.mean(xf * xf, axis=-1, keepdims=True) + EPS) * g


def _sigmoid(v):
    return 1.0 / (1.0 + jnp.exp(-v))


def _full(shape):
    return pl.BlockSpec(shape, lambda *_: (0,) * len(shape))


def _params(n_axes):
    return pltpu.CompilerParams(dimension_semantics=("arbitrary",) * n_axes,
                                vmem_limit_bytes=VMEM_LIMIT)


def _inproj_kernel(x_ref, gmix_ref, wna_ref, wcq_ref, wckv_ref, wkr_ref, wqm_ref, qn_ref, kvn_ref,
                   wuq_ref, wukv_ref, cos_ref, sin_ref,
                   na_ref, qmla_ref, kmla_ref, vmla_ref, qmem_ref, *, mla_scale, mem_scale):
    n = _rms(x_ref[...], gmix_ref[...]).astype(BF16)
    z_na = _dot(n, wna_ref[...])
    na_ref[:, :NA_W] = (z_na[:, :NA_W] * (NA_HEAD_DIM ** -0.5)).astype(BF16)
    na_ref[:, NA_W:] = z_na[:, NA_W:].astype(BF16)
    qmem_ref[...] = (_dot(n, wqm_ref[...]) * mem_scale).astype(BF16)

    cos = cos_ref[...]
    sin = sin_ref[...]
    zkr = _dot(n, wkr_ref[...])
    kpe = zkr[:, :LANES] * cos + zkr[:, LANES:] * sin

    cqn = _rms(_dot(n, wcq_ref[...]), qn_ref[...]).astype(BF16)
    zq = _dot(cqn, wuq_ref[...])
    hw = MLA_HEADS * LANES
    for h in range(MLA_HEADS):
        a = zq[:, h * LANES:(h + 1) * LANES]
        b = zq[:, hw + h * LANES:hw + (h + 1) * LANES]
        qmla_ref[:, h * LANES:(h + 1) * LANES] = ((a * cos + b * sin) * mla_scale).astype(BF16)

    ckvn = _rms(_dot(n, wckv_ref[...]), kvn_ref[...]).astype(BF16)
    zkv = _dot(ckvn, wukv_ref[...])
    for h in range(MLA_HEADS):
        kmla_ref[:, h * LANES:(h + 1) * LANES] = (zkv[:, h * LANES:(h + 1) * LANES] + kpe).astype(BF16)
    vmla_ref[...] = zkv[:, hw:].astype(BF16)


def _inproj(x2, gmix, wna, wcq, wckv, wkr, wqm, qn, kvn, wuq, wukv, cos, sin, *, seq, tm):
    t, d = x2.shape
    nblk = seq // tm
    row = lambda w: pl.BlockSpec((tm, w), lambda i: (i, 0))
    rope = pl.BlockSpec((tm, LANES), lambda i: (i % nblk, 0))
    hw = MLA_HEADS * LANES
    kern = functools.partial(_inproj_kernel, mla_scale=(MLA_NOPE + MLA_ROPE) ** -0.5,
                             mem_scale=MEM_HEAD_DIM ** -0.5)
    return pl.pallas_call(
        kern,
        grid=(t // tm,),
        in_specs=[row(d), _full(gmix.shape), _full(wna.shape), _full(wcq.shape), _full(wckv.shape),
                  _full(wkr.shape), _full(wqm.shape), _full(qn.shape), _full(kvn.shape),
                  _full(wuq.shape), _full(wukv.shape), rope, rope],
        out_specs=[row(3 * NA_W), row(hw), row(hw), row(MLA_HEADS * MLA_V), row(MEM_W)],
        out_shape=[jax.ShapeDtypeStruct((t, 3 * NA_W), BF16),
                   jax.ShapeDtypeStruct((t, hw), BF16),
                   jax.ShapeDtypeStruct((t, hw), BF16),
                   jax.ShapeDtypeStruct((t, MLA_HEADS * MLA_V), BF16),
                   jax.ShapeDtypeStruct((t, MEM_W), BF16)],
        compiler_params=_params(1),
        name="inproj",
    )(x2, gmix, wna, wcq, wckv, wkr, wqm, qn, kvn, wuq, wukv, cos, sin)


NA_QROWS = NA_WIN_R // 2
NA_BAND = 12


def _na_kernel(q_ref, k_ref, v_ref, bias_ref, y_ref, *, rows):
    i = pl.program_id(1)
    us = jnp.clip(i * NA_QROWS - NA_WIN_R // 2, 0, rows - NA_BAND)
    start = pl.multiple_of(us * GRID_W, GRID_W)
    band = NA_BAND * GRID_W
    lane = lax.broadcasted_iota(jnp.int32, (NA_QROWS * GRID_W, LANES), 1)
    first = lane < NA_HEAD_DIM
    for hp in range(NA_HEADS // 2):
        cols = slice(hp * LANES, (hp + 1) * LANES)
        qp = q_ref[:, cols]
        kp = k_ref[pl.ds(start, band), cols]
        vp = v_ref[pl.ds(start, band), cols]
        outs = []
        for s in range(2):
            qm = jnp.where(first if s == 0 else ~first, qp, jnp.zeros_like(qp))
            lg = _dot_nt(qm, kp) + bias_ref[0, 2 * hp + s]
            p = jnp.exp(lg - jnp.max(lg, axis=1, keepdims=True))
            l = jnp.sum(p, axis=1, keepdims=True)
            outs.append(_dot(p.astype(BF16), vp) * (1.0 / l))
        y_ref[:, cols] = jnp.where(first, outs[0], outs[1]).astype(BF16)


def _na_bias_table(rpb):
    q, bn, wr, wc = NA_QROWS, NA_BAND, NA_WIN_R, NA_WIN_C
    rho = np.arange(q)
    own = np.stack([0 * rho, rho, 0 * rho + (bn - wr)])
    dd = np.stack([rho, 0 * rho + wr // 2, rho + wr // 2])
    ip = np.arange(bn)[None, None, :] - own[:, :, None]
    rvalid = (ip >= 0) & (ip < wr)
    dr = ip - dd[:, :, None] + (wr - 1)
    rsel = (dr[..., None] == np.arange(2 * wr - 1)) & rvalid[..., None]
    c = np.arange(GRID_W)[:, None]
    j = np.arange(GRID_W)[None, :]
    cstart = np.clip(c - wc // 2, 0, GRID_W - wc)
    cvalid = (j >= cstart) & (j < cstart + wc)
    csel = ((j - c + wc - 1)[None] == np.arange(2 * wc - 1)[:, None, None]) & cvalid[None]
    tab = jnp.einsum("hrk,vpir,kcj->vhpcij", rpb.astype(F32), jnp.asarray(rsel, F32), jnp.asarray(csel, F32),
                     precision=lax.Precision.HIGHEST)
    valid = rvalid[:, None, :, None, :, None] & cvalid[None, None, None, :, None, :]
    tab = jnp.where(jnp.asarray(valid), tab, NEG)
    return tab.reshape(3, NA_HEADS, q * GRID_W, bn * GRID_W)


def _na_attention(na, bias, *, batch, seq):
    rows = seq // GRID_W
    nblk = rows // NA_QROWS
    assert rows % NA_QROWS == 0 and rows >= NA_BAND and NA_BAND >= NA_QROWS + NA_WIN_R - 1
    nq = NA_QROWS * GRID_W

    def bias_map(b, i):
        return (jnp.where(i == 0, 0, jnp.where(i == nblk - 1, 2, 1)), 0, 0, 0)

    return pl.pallas_call(
        functools.partial(_na_kernel, rows=rows),
        grid=(batch, nblk),
        in_specs=[pl.BlockSpec((nq, NA_W), lambda b, i: (b * nblk + i, 0)),
                  pl.BlockSpec((seq, NA_W), lambda b, i: (b, 1)),
                  pl.BlockSpec((seq, NA_W), lambda b, i: (b, 2)),
                  pl.BlockSpec((1, NA_HEADS, nq, NA_BAND * GRID_W), bias_map)],
        out_specs=pl.BlockSpec((nq, NA_W), lambda b, i: (b * nblk + i, 0)),
        out_shape=jax.ShapeDtypeStruct((batch * seq, NA_W), BF16),
        compiler_params=_params(2),
        name="na_attn",
    )(na, na, na, bias)


def _mla_kernel(q_ref, k_ref, v_ref, o_ref):
    tq = q_ref.shape[0]
    first = lax.broadcasted_iota(jnp.int32, (tq, LANES), 1) < MLA_V
    v = v_ref[...]
    outs = []
    for s in range(2):
        cols = slice(s * LANES, (s + 1) * LANES)
        lg = _dot_nt(q_ref[:, cols], k_ref[:, cols])
        p = jnp.exp(lg - jnp.max(lg, axis=1, keepdims=True))
        l = jnp.sum(p, axis=1, keepdims=True)
        outs.append(_dot(p.astype(BF16), v) * (1.0 / l))
    o_ref[...] = jnp.where(first, outs[0], outs[1]).astype(BF16)


def _mla_attention(q, k, v, *, batch, seq, tq):
    nq = seq // tq
    hp = MLA_HEADS // 2
    return pl.pallas_call(
        _mla_kernel,
        grid=(batch, hp, nq),
        in_specs=[pl.BlockSpec((tq, 2 * LANES), lambda b, h, i: (b * nq + i, h)),
                  pl.BlockSpec((seq, 2 * LANES), lambda b, h, i: (b, h)),
                  pl.BlockSpec((seq, 2 * MLA_V), lambda b, h, i: (b, h))],
        out_specs=pl.BlockSpec((tq, 2 * MLA_V), lambda b, h, i: (b * nq + i, h)),
        out_shape=jax.ShapeDtypeStruct((batch * seq, MLA_HEADS * MLA_V), BF16),
        compiler_params=_params(3),
        name="mla_attn",
    )(q, k, v)


def _memkv_kernel(mem_ref, g_ref, w_ref, kv_ref):
    kv_ref[...] = _dot(_rms(mem_ref[...], g_ref[...]).astype(BF16), w_ref[...]).astype(BF16)


def _memkv(mem2, g, w, *, batch, n_mem):
    d = mem2.shape[1]
    return pl.pallas_call(
        _memkv_kernel,
        grid=(batch,),
        in_specs=[pl.BlockSpec((n_mem, d), lambda b: (b, 0)), _full(g.shape), _full(w.shape)],
        out_specs=pl.BlockSpec((n_mem, 2 * MEM_W), lambda b: (b, 0)),
        out_shape=jax.ShapeDtypeStruct((batch * n_mem, 2 * MEM_W), BF16),
        compiler_params=_params(1),
        name="memkv",
    )(mem2, g, w)


def _merge_kernel(x_ref, yna_ref, ymla_ref, qmem_ref, kvm_ref, gmix_ref, wg_ref, wona_ref, womla_ref,
                  womem_ref, wout_ref, gffn_ref, wrh_ref, wrl_ref, br_ref, h1_ref, xn_ref, lg_ref):
    x = x_ref[...]
    d = x.shape[1]
    n = _rms(x, gmix_ref[...]).astype(BF16)

    ymem = []
    for h in range(MEM_HEADS):
        cols = slice(h * LANES, (h + 1) * LANES)
        lg = _dot_nt(qmem_ref[:, cols], kvm_ref[:, cols])
        p = jnp.exp(lg - jnp.max(lg, axis=1, keepdims=True))
        l = jnp.sum(p, axis=1, keepdims=True)
        vh = kvm_ref[:, MEM_W + h * LANES:MEM_W + (h + 1) * LANES]
        ymem.append((_dot(p.astype(BF16), vh) * (1.0 / l)).astype(BF16))
    ymem = jnp.concatenate(ymem, axis=1)

    merged = None
    for j, (y, w_ref) in enumerate(((yna_ref[...], wona_ref), (ymla_ref[...], womla_ref), (ymem, womem_ref))):
        gate = _sigmoid(_dot(n, wg_ref[:, j * d:(j + 1) * d]))
        term = gate * _dot(y, w_ref[...])
        merged = term if merged is None else merged + term
    h1 = x + _dot(merged.astype(BF16), wout_ref[...])
    h1_ref[...] = h1

    xn = _rms(h1, gffn_ref[...])
    xhi = xn.astype(BF16)
    xlo = (xn - xhi.astype(F32)).astype(BF16)
    lg_ref[...] = (_dot(xhi, wrh_ref[...]) + (_dot(xhi, wrl_ref[...]) + _dot(xlo, wrh_ref[...]))
                   + br_ref[...])
    xn_ref[...] = xn


def _merge(x2, yna, ymla, qmem, kvm, gmix, wg, wona, womla, womem, wout, gffn, wrh, wrl, br, *, seq, n_mem, tm):
    t, d = x2.shape
    nblk = seq // tm
    row = lambda w: pl.BlockSpec((tm, w), lambda i: (i, 0))
    return pl.pallas_call(
        _merge_kernel,
        grid=(t // tm,),
        in_specs=[row(d), row(NA_W), row(MLA_HEADS * MLA_V), row(MEM_W),
                  pl.BlockSpec((n_mem, 2 * MEM_W), lambda i: (i // nblk, 0)),
                  _full(gmix.shape), _full(wg.shape), _full(wona.shape), _full(womla.shape),
                  _full(womem.shape), _full(wout.shape), _full(gffn.shape), _full(wrh.shape),
                  _full(wrl.shape), _full(br.shape)],
        out_specs=[row(d), row(d), row(LANES)],
        out_shape=[jax.ShapeDtypeStruct((t, d), F32),
                   jax.ShapeDtypeStruct((t, d), F32),
                   jax.ShapeDtypeStruct((t, LANES), F32)],
        compiler_params=_params(1),
        name="merge",
    )(x2, yna, ymla, qmem, kvm, gmix, wg, wona, womla, womem, wout, gffn, wrh, wrl, br)


def _route_kernel(lg_ref, idx_ref, aff_ref, bnd_ref, cs_ref, *, cap, nsplit, tb):
    s = lg_ref.shape[0]
    lg = lg_ref[...]
    ex = jnp.exp(lg - jnp.max(lg, axis=1, keepdims=True))
    aff = ex / jnp.sum(ex, axis=1, keepdims=True)
    aff_ref[...] = aff

    aff_t = aff.T[:N_EXPERTS, :]

    def search(i, lo):
        cand = lo | lax.shift_left(jnp.int32(1), 30 - i)
        cnt = jnp.sum((aff_t >= lax.bitcast_convert_type(cand, F32)).astype(jnp.int32), axis=1, keepdims=True)
        return jnp.where(cnt >= cap, cand, lo)

    thr_col = lax.bitcast_convert_type(
        lax.fori_loop(0, 31, search, jnp.zeros((N_EXPERTS, 1), jnp.int32)), F32)
    thr_sq = jnp.concatenate([jnp.broadcast_to(thr_col, (N_EXPERTS, LANES)),
                              jnp.zeros((LANES - N_EXPERTS, LANES), F32)], axis=0)
    thr = thr_sq.T[0:1, :]
    gt = aff > thr
    eq = aff == thr
    need = cap - jnp.sum(gt.astype(jnp.int32), axis=0, keepdims=True)

    tri = (lax.broadcasted_iota(jnp.int32, (tb, tb), 0)
           >= lax.broadcasted_iota(jnp.int32, (tb, tb), 1)).astype(BF16)

    def cumsum_blocks(mask_of_block):
        carry = jnp.zeros((1, LANES), F32)
        for blk in range(s // tb):
            rows = slice(blk * tb, (blk + 1) * tb)
            c = _dot(tri, mask_of_block(rows).astype(BF16)) + carry
            cs_ref[rows, :] = c
            carry = c[tb - 1:tb, :]

    cumsum_blocks(lambda rows: eq[rows, :])
    eq_rank = cs_ref[...] - eq.astype(F32)
    sel = gt | (eq & (eq_rank < need.astype(F32)))
    cumsum_blocks(lambda rows: sel[rows, :])

    sq = s // nsplit
    sub = lax.broadcasted_iota(jnp.int32, (8, LANES), 0)
    bnd = jnp.zeros((8, LANES), jnp.int32)
    for k in range(1, nsplit + 1):
        bnd = jnp.where(sub == k, cs_ref[k * sq - 1:k * sq, :].astype(jnp.int32), bnd)
    bnd_ref[0] = bnd

    jrow = lax.broadcasted_iota(jnp.int32, (1, cap), 1).astype(F32)
    ones = jnp.ones((8, tb), BF16)
    for e in range(N_EXPERTS):
        cnt = jnp.zeros((8, cap), F32)
        for blk in range(s // tb):
            col = cs_ref[blk * tb:(blk + 1) * tb, e:e + 1]
            cnt = cnt + _dot(ones, jnp.where(col <= jrow, 1.0, 0.0).astype(BF16))
        idx_ref[0, e:e + 1, :] = cnt[0:1, :].astype(jnp.int32)


def _route(lg, *, batch, seq, cap, nsplit):
    tb = min(512, seq)
    return pl.pallas_call(
        functools.partial(_route_kernel, cap=cap, nsplit=nsplit, tb=tb),
        grid=(batch,),
        in_specs=[pl.BlockSpec((seq, LANES), lambda b: (b, 0))],
        out_specs=[pl.BlockSpec((1, N_EXPERTS, cap), lambda b: (b, 0, 0)),
                   pl.BlockSpec((seq, LANES), lambda b: (b, 0)),
                   pl.BlockSpec((1, 8, LANES), lambda b: (b, 0, 0))],
        out_shape=[jax.ShapeDtypeStruct((batch, N_EXPERTS, cap), jnp.int32),
                   jax.ShapeDtypeStruct((batch * seq, LANES), F32),
                   jax.ShapeDtypeStruct((batch, 8, LANES), jnp.int32)],
        scratch_shapes=[pltpu.VMEM((seq, LANES), F32)],
        compiler_params=_params(1),
        name="route",
    )(lg)


def _expert_kernel(idx_ref, xn_ref, aff_ref, wg_ref, wu_ref, wd_ref, yg_ref, xs_ref, ga_ref, *, cap):
    e = pl.program_id(1)

    def gather(i, c):
        t = idx_ref[0, 0, i]
        xs_ref[pl.ds(i, 1), :] = xn_ref[pl.ds(t, 1), :]
        ga_ref[pl.ds(i, 1), :] = aff_ref[pl.ds(t, 1), :]
        return c

    lax.fori_loop(0, cap, gather, 0, unroll=8)

    xs = xs_ref[...].astype(BF16)
    hg = _dot(xs, wg_ref[0])
    hu = _dot(xs, wu_ref[0])
    act = (hg * _sigmoid(hg) * hu).astype(BF16)
    y = _dot(act, wd_ref[0])
    lane = lax.broadcasted_iota(jnp.int32, ga_ref.shape, 1)
    g = jnp.sum(jnp.where(lane == e, ga_ref[...], 0.0), axis=1, keepdims=True)
    yg_ref[0] = y * g


def _experts(idx3, xn, aff, wg, wu, wd, *, batch, seq, cap):
    d = wg.shape[1]
    ff = wg.shape[2]
    ne = N_EXPERTS
    once = pl.Buffered(1)
    return pl.pallas_call(
        functools.partial(_expert_kernel, cap=cap),
        grid=(batch, ne),
        in_specs=[pl.BlockSpec((1, 1, cap), lambda b, e: (b * ne + e, 0, 0), memory_space=pltpu.SMEM),
                  pl.BlockSpec((seq, d), lambda b, e: (b, 0), pipeline_mode=once),
                  pl.BlockSpec((seq, LANES), lambda b, e: (b, 0), pipeline_mode=once),
                  pl.BlockSpec((1, d, ff), lambda b, e: (e, 0, 0)),
                  pl.BlockSpec((1, d, ff), lambda b, e: (e, 0, 0)),
                  pl.BlockSpec((1, ff, d), lambda b, e: (e, 0, 0))],
        out_specs=pl.BlockSpec((1, cap, d), lambda b, e: (b * ne + e, 0, 0)),
        out_shape=jax.ShapeDtypeStruct((batch * ne, cap, d), F32),
        scratch_shapes=[pltpu.VMEM((cap, d), F32), pltpu.VMEM((cap, LANES), F32)],
        compiler_params=_params(2),
        name="experts",
    )(idx3, xn, aff, wg, wu, wd)


def _combine_kernel(idx_ref, bnd_ref, h1_ref, yg_ref, gfin_ref, out_ref, *, sq):
    q = pl.program_id(1)
    e = pl.program_id(2)

    @pl.when(e == 0)
    def _():
        out_ref[...] = h1_ref[...]

    base = q * sq

    def scatter(i, c):
        t = idx_ref[0, 0, i] - base
        out_ref[pl.ds(t, 1), :] += yg_ref[0, pl.ds(i, 1), :]
        return c

    def scatter4(k, c):
        i = lo + 4 * k
        ts = [idx_ref[0, 0, i + u] - base for u in range(4)]
        new = [out_ref[pl.ds(ts[u], 1), :] + yg_ref[0, pl.ds(i + u, 1), :] for u in range(4)]
        for u in range(4):
            out_ref[pl.ds(ts[u], 1), :] = new[u]
        return c

    lo = bnd_ref[0, q, e]
    hi = bnd_ref[0, q + 1, e]
    quads = lax.shift_right_logical(hi - lo, 2)
    lax.fori_loop(0, quads, scatter4, 0)
    lax.fori_loop(lo + 4 * quads, hi, scatter, 0)

    @pl.when(e == pl.num_programs(2) - 1)
    def _():
        out_ref[...] = _rms(out_ref[...], gfin_ref[...])


def _combine(idx3, bnd, h1, yg, gfin, *, batch, seq, cap, nsplit):
    d = h1.shape[1]
    ne = N_EXPERTS
    sq = seq // nsplit
    return pl.pallas_call(
        functools.partial(_combine_kernel, sq=sq),
        grid=(batch, nsplit, ne),
        in_specs=[pl.BlockSpec((1, 1, cap), lambda b, q, e: (b * ne + e, 0, 0), memory_space=pltpu.SMEM),
                  pl.BlockSpec((1, 8, LANES), lambda b, q, e: (b, 0, 0), memory_space=pltpu.SMEM),
                  pl.BlockSpec((sq, d), lambda b, q, e: (b * nsplit + q, 0)),
                  pl.BlockSpec((1, cap, d), lambda b, q, e: (b * ne + e, 0, 0)),
                  _full(gfin.shape)],
        out_specs=pl.BlockSpec((sq, d), lambda b, q, e: (b * nsplit + q, 0)),
        out_shape=jax.ShapeDtypeStruct((batch * seq, d), F32),
        compiler_params=_params(3),
        name="combine",
    )(idx3, bnd, h1, yg, gfin)


def _rope_tables(seq):
    t = jnp.arange(seq)
    half = MLA_ROPE // 2
    inv = ROPE_BASE ** (-jnp.arange(0, half, 2, dtype=F32) / half)
    ang_r = (t // GRID_W).astype(F32)[:, None] * inv[None, :]
    ang_c = (t % GRID_W).astype(F32)[:, None] * inv[None, :]
    cos = jnp.concatenate([jnp.cos(ang_r), jnp.cos(ang_r), jnp.cos(ang_c), jnp.cos(ang_c)], axis=1)
    sin = jnp.concatenate([-jnp.sin(ang_r), jnp.sin(ang_r), -jnp.sin(ang_c), jnp.sin(ang_c)], axis=1)
    pad = LANES - MLA_NOPE - MLA_ROPE
    cos = jnp.concatenate([jnp.ones((seq, MLA_NOPE), F32), cos, jnp.zeros((seq, pad), F32)], axis=1)
    sin = jnp.concatenate([jnp.zeros((seq, MLA_NOPE), F32), sin, jnp.zeros((seq, pad), F32)], axis=1)
    return cos, sin


_q = MLA_ROPE // 4
ROPE_SWAP = np.concatenate([np.arange(_q, 2 * _q), np.arange(0, _q),
                            np.arange(3 * _q, 4 * _q), np.arange(2 * _q, 3 * _q)])


def kernel(x, mem, g_mix, w_in, na_rpb, mla_q_norm, w_mla_uq, mla_kv_norm, w_mla_ukv, g_mem, w_mem_kv,
           w_o_na, w_o_mla, w_o_mem, w_out, g_ffn, w_router, b_router, w_gate, w_up, w_down, g_final):
    batch, seq, d = x.shape
    n_mem = mem.shape[1]
    depth = g_mix.shape[0]
    q_rank = mla_q_norm.shape[1]
    kv_rank = mla_kv_norm.shape[1]
    cap = EC_FACTOR * seq // N_EXPERTS
    nsplit = 2
    tm = 512
    pad = LANES - MLA_NOPE - MLA_ROPE
    cos, sin = _rope_tables(seq)

    assert depth == 1
    h = x.reshape(batch * seq, d)
    for l in range(depth):
        w = w_in[l]
        o = np.cumsum([0, NA_W, NA_W, NA_W, q_rank, kv_rank, MLA_ROPE, MEM_W, 3 * d])
        wna = w[:, o[0]:o[3]].astype(BF16)
        wcq = w[:, o[3]:o[4]].astype(BF16)
        wckv = w[:, o[4]:o[5]].astype(BF16)
        wkr_raw = w[:, o[5]:o[6]]
        wqm = w[:, o[6]:o[7]].astype(BF16)
        wgate_cols = w[:, o[7]:o[8]].astype(BF16)
        z = lambda *s: jnp.zeros(s, F32)
        wkr = jnp.concatenate([z(d, MLA_NOPE), wkr_raw, z(d, pad),
                               z(d, MLA_NOPE), wkr_raw[:, ROPE_SWAP], z(d, pad)], axis=1).astype(BF16)
        uq = w_mla_uq[l].reshape(q_rank, MLA_HEADS, MLA_NOPE + MLA_ROPE)
        uq_a = jnp.concatenate([uq, z(q_rank, MLA_HEADS, pad)], axis=2)
        uq_b = jnp.concatenate([z(q_rank, MLA_HEADS, MLA_NOPE), uq[:, :, MLA_NOPE:][:, :, ROPE_SWAP],
                                z(q_rank, MLA_HEADS, pad)], axis=2)
        wuq = jnp.concatenate([uq_a.reshape(q_rank, -1), uq_b.reshape(q_rank, -1)], axis=1).astype(BF16)
        ukv = w_mla_ukv[l].reshape(kv_rank, MLA_HEADS, MLA_NOPE + MLA_V)
        uk = jnp.concatenate([ukv[:, :, :MLA_NOPE], z(kv_rank, MLA_HEADS, LANES - MLA_NOPE)], axis=2)
        wukv = jnp.concatenate([uk.reshape(kv_rank, -1), ukv[:, :, MLA_NOPE:].reshape(kv_rank, -1)],
                               axis=1).astype(BF16)
        wr = jnp.concatenate([w_router[l].astype(F32), z(d, LANES - N_EXPERTS)], axis=1)
        wrh = wr.astype(BF16)
        wrl = (wr - wrh.astype(F32)).astype(BF16)
        br = jnp.concatenate([b_router[l].astype(F32), jnp.full((LANES - N_EXPERTS,), NEG, F32)])[None, :]

        na, qmla, kmla, vmla, qmem = _inproj(
            h, g_mix[l][None, :], wna, wcq, wckv, wkr, wqm, mla_q_norm[l][None, :], mla_kv_norm[l][None, :],
            wuq, wukv, cos, sin, seq=seq, tm=tm)
        y_na = _na_attention(na, _na_bias_table(na_rpb[l]), batch=batch, seq=seq)
        y_mla = _mla_attention(qmla, kmla, vmla, batch=batch, seq=seq, tq=256)
        kvm = _memkv(mem.reshape(batch * n_mem, d), g_mem[l][None, :], w_mem_kv[l].astype(BF16),
                     batch=batch, n_mem=n_mem)
        h1, xn, lg = _merge(h, y_na, y_mla, qmem, kvm, g_mix[l][None, :], wgate_cols,
                             w_o_na[l].astype(BF16), w_o_mla[l].astype(BF16), w_o_mem[l].astype(BF16),
                             w_out[l].astype(BF16), g_ffn[l][None, :], wrh, wrl, br,
                             seq=seq, n_mem=n_mem, tm=256)
        idx, aff, bnd = _route(lg, batch=batch, seq=seq, cap=cap, nsplit=nsplit)
        idx3 = idx.reshape(batch * N_EXPERTS, 1, cap)
        yg = _experts(idx3, xn, aff, w_gate[l].astype(BF16), w_up[l].astype(BF16), w_down[l].astype(BF16),
                      batch=batch, seq=seq, cap=cap)
        h = _combine(idx3, bnd, h1, yg, g_final[None, :], batch=batch, seq=seq, cap=cap, nsplit=nsplit)
    return h.reshape(batch, seq, d)
```

```python
import functools

import jax
import jax.numpy as jnp
import numpy as np
from jax import lax
from jax.experimental import pallas as pl
from jax.experimental.pallas import tpu as pltpu

GRID_W = 64
EPS = 1e-6
NA_HEADS = 8
NA_HEAD_DIM = 64
NA_WIN_R = 8
NA_WIN_C = 16
MLA_HEADS = 8
MLA_NOPE = 64
MLA_ROPE = 32
MLA_V = 64
ROPE_BASE = 10000.0
MEM_HEADS = 4
MEM_HEAD_DIM = 128
N_EXPERTS = 16
EC_FACTOR = 2
NA_W = NA_HEADS * NA_HEAD_DIM
MEM_W = MEM_HEADS * MEM_HEAD_DIM
LANES = 128
NEG = -1e30
VMEM_LIMIT = 56 * 1024 * 1024

F32 = jnp.float32
BF16 = jnp.bfloat16


def _dot(a, b):
    return jnp.dot(a, b, preferred_element_type=F32)


def _dot_nt(a, b):
    return lax.dot_general(a, b, (((1,), (1,)), ((), ())), preferred_element_type=F32)


def _rms(xf, g):
    return xf * lax.rsqrt(jnp.mean(xf * xf, axis=-1, keepdims=True) + EPS) * g


def _sigmoid(v):
    return 1.0 / (1.0 + jnp.exp(-v))


def _full(shape):
    return pl.BlockSpec(shape, lambda *_: (0,) * len(shape))


def _params(n_axes):
    return pltpu.CompilerParams(dimension_semantics=("arbitrary",) * n_axes,
                                vmem_limit_bytes=VMEM_LIMIT)


def _inproj_kernel(x_ref, gmix_ref, wna_ref, wcq_ref, wckv_ref, wkr_ref, wqm_ref, qn_ref, kvn_ref,
                   wuq_ref, wukv_ref, cos_ref, sin_ref,
                   na_ref, qmla_ref, kmla_ref, vmla_ref, qmem_ref, *, mla_scale, mem_scale):
    n = _rms(x_ref[...], gmix_ref[...]).astype(BF16)
    z_na = _dot(n, wna_ref[...])
    na_ref[:, :NA_W] = (z_na[:, :NA_W] * (NA_HEAD_DIM ** -0.5)).astype(BF16)
    na_ref[:, NA_W:] = z_na[:, NA_W:].astype(BF16)
    qmem_ref[...] = (_dot(n, wqm_ref[...]) * mem_scale).astype(BF16)

    cos = cos_ref[...]
    sin = sin_ref[...]
    zkr = _dot(n, wkr_ref[...])
    kpe = zkr[:, :LANES] * cos + zkr[:, LANES:] * sin

    cqn = _rms(_dot(n, wcq_ref[...]), qn_ref[...]).astype(BF16)
    zq = _dot(cqn, wuq_ref[...])
    hw = MLA_HEADS * LANES
    for h in range(MLA_HEADS):
        a = zq[:, h * LANES:(h + 1) * LANES]
        b = zq[:, hw + h * LANES:hw + (h + 1) * LANES]
        qmla_ref[:, h * LANES:(h + 1) * LANES] = ((a * cos + b * sin) * mla_scale).astype(BF16)

    ckvn = _rms(_dot(n, wckv_ref[...]), kvn_ref[...]).astype(BF16)
    zkv = _dot(ckvn, wukv_ref[...])
    for h in range(MLA_HEADS):
        kmla_ref[:, h * LANES:(h + 1) * LANES] = (zkv[:, h * LANES:(h + 1) * LANES] + kpe).astype(BF16)
    is_v = lax.broadcasted_iota(jnp.int32, (zkv.shape[0], LANES), 1) < MLA_V
    for h in range(MLA_HEADS):
        blk = zkv[:, hw + h * LANES:hw + (h + 1) * LANES]
        vmla_ref[:, h * LANES:(h + 1) * LANES] = jnp.where(is_v, blk, 1.0).astype(BF16)


def _inproj(x2, gmix, wna, wcq, wckv, wkr, wqm, qn, kvn, wuq, wukv, cos, sin, *, seq, tm):
    t, d = x2.shape
    nblk = seq // tm
    row = lambda w: pl.BlockSpec((tm, w), lambda i: (i, 0))
    rope = pl.BlockSpec((tm, LANES), lambda i: (i % nblk, 0))
    hw = MLA_HEADS * LANES
    kern = functools.partial(_inproj_kernel, mla_scale=(MLA_NOPE + MLA_ROPE) ** -0.5,
                             mem_scale=MEM_HEAD_DIM ** -0.5)
    return pl.pallas_call(
        kern,
        grid=(t // tm,),
        in_specs=[row(d), _full(gmix.shape), _full(wna.shape), _full(wcq.shape), _full(wckv.shape),
                  _full(wkr.shape), _full(wqm.shape), _full(qn.shape), _full(kvn.shape),
                  _full(wuq.shape), _full(wukv.shape), rope, rope],
        out_specs=[row(3 * NA_W), row(hw), row(hw), row(hw), row(MEM_W)],
        out_shape=[jax.ShapeDtypeStruct((t, 3 * NA_W), BF16),
                   jax.ShapeDtypeStruct((t, hw), BF16),
                   jax.ShapeDtypeStruct((t, hw), BF16),
                   jax.ShapeDtypeStruct((t, hw), BF16),
                   jax.ShapeDtypeStruct((t, MEM_W), BF16)],
        compiler_params=_params(1),
        name="inproj",
    )(x2, gmix, wna, wcq, wckv, wkr, wqm, qn, kvn, wuq, wukv, cos, sin)


NA_QROWS = NA_WIN_R // 2
NA_BAND = 12


def _na_kernel(q_ref, k_ref, v_ref, bias_ref, y_ref, *, rows):
    i = pl.program_id(1)
    us = jnp.clip(i * NA_QROWS - NA_WIN_R // 2, 0, rows - NA_BAND)
    start = pl.multiple_of(us * GRID_W, GRID_W)
    band = NA_BAND * GRID_W
    first = lax.broadcasted_iota(jnp.int32, (NA_QROWS * GRID_W, LANES), 1) < NA_HEAD_DIM
    kfirst = lax.broadcasted_iota(jnp.int32, (band, LANES), 1) < NA_HEAD_DIM
    for hp in range(NA_HEADS // 2):
        cols = slice(hp * LANES, (hp + 1) * LANES)
        qp = q_ref[:, cols]
        kp = k_ref[pl.ds(start, band), cols]
        vp = v_ref[pl.ds(start, band), cols]
        outs = []
        for s in range(2):
            mine = first if s == 0 else ~first
            qm = jnp.where(mine, qp, jnp.zeros_like(qp))
            lg = _dot_nt(qm, kp) + bias_ref[0, 2 * hp + s]
            p = jnp.exp((lg - jnp.max(lg, axis=1, keepdims=True)).astype(BF16))
            oa = _dot(p, jnp.where(kfirst if s == 0 else ~kfirst, vp, jnp.ones_like(vp)))
            other = (1 - s) * NA_HEAD_DIM
            outs.append(oa * (1.0 / oa[:, other:other + 1]))
        y_ref[:, cols] = jnp.where(first, outs[0], outs[1]).astype(BF16)


def _na_bias_table(rpb):
    q, bn, wr, wc = NA_QROWS, NA_BAND, NA_WIN_R, NA_WIN_C
    rho = np.arange(q)
    own = np.stack([0 * rho, rho, 0 * rho + (bn - wr)])
    dd = np.stack([rho, 0 * rho + wr // 2, rho + wr // 2])
    ip = np.arange(bn)[None, None, :] - own[:, :, None]
    rvalid = (ip >= 0) & (ip < wr)
    dr = ip - dd[:, :, None] + (wr - 1)
    rsel = (dr[..., None] == np.arange(2 * wr - 1)) & rvalid[..., None]
    c = np.arange(GRID_W)[:, None]
    j = np.arange(GRID_W)[None, :]
    cstart = np.clip(c - wc // 2, 0, GRID_W - wc)
    cvalid = (j >= cstart) & (j < cstart + wc)
    csel = ((j - c + wc - 1)[None] == np.arange(2 * wc - 1)[:, None, None]) & cvalid[None]
    tab = jnp.einsum("hrk,vpir,kcj->vhpcij", rpb.astype(F32), jnp.asarray(rsel, F32), jnp.asarray(csel, F32),
                     precision=lax.Precision.HIGHEST)
    valid = rvalid[:, None, :, None, :, None] & cvalid[None, None, None, :, None, :]
    tab = jnp.where(jnp.asarray(valid), tab, NEG)
    return tab.reshape(3, NA_HEADS, q * GRID_W, bn * GRID_W)


def _na_attention(na, bias, *, batch, seq):
    rows = seq // GRID_W
    nblk = rows // NA_QROWS
    assert rows % NA_QROWS == 0 and rows >= NA_BAND and NA_BAND >= NA_QROWS + NA_WIN_R - 1
    nq = NA_QROWS * GRID_W

    def bias_map(b, i):
        return (jnp.where(i == 0, 0, jnp.where(i == nblk - 1, 2, 1)), 0, 0, 0)

    return pl.pallas_call(
        functools.partial(_na_kernel, rows=rows),
        grid=(batch, nblk),
        in_specs=[pl.BlockSpec((nq, NA_W), lambda b, i: (b * nblk + i, 0)),
                  pl.BlockSpec((seq, NA_W), lambda b, i: (b, 1)),
                  pl.BlockSpec((seq, NA_W), lambda b, i: (b, 2)),
                  pl.BlockSpec((1, NA_HEADS, nq, NA_BAND * GRID_W), bias_map)],
        out_specs=pl.BlockSpec((nq, NA_W), lambda b, i: (b * nblk + i, 0)),
        out_shape=jax.ShapeDtypeStruct((batch * seq, NA_W), BF16),
        compiler_params=_params(2),
        name="na_attn",
    )(na, na, na, bias)


def _mla_kernel(q_ref, k_ref, v_ref, o_ref):
    tq = q_ref.shape[0]
    first = lax.broadcasted_iota(jnp.int32, (tq, LANES), 1) < MLA_V
    outs = []
    for s in range(2):
        cols = slice(s * LANES, (s + 1) * LANES)
        lg = _dot_nt(q_ref[:, cols], k_ref[:, cols])
        p = jnp.exp((lg - jnp.max(lg, axis=1, keepdims=True)).astype(BF16))
        oa = _dot(p, v_ref[:, cols])
        outs.append(oa * (1.0 / oa[:, MLA_V:MLA_V + 1]))
    o_ref[...] = jnp.where(first, outs[0], pltpu.roll(outs[1], MLA_V, 1)).astype(BF16)


def _mla_attention(q, k, v, *, batch, seq, tq):
    nq = seq // tq
    hp = MLA_HEADS // 2
    return pl.pallas_call(
        _mla_kernel,
        grid=(batch, hp, nq),
        in_specs=[pl.BlockSpec((tq, 2 * LANES), lambda b, h, i: (b * nq + i, h)),
                  pl.BlockSpec((seq, 2 * LANES), lambda b, h, i: (b, h)),
                  pl.BlockSpec((seq, 2 * LANES), lambda b, h, i: (b, h))],
        out_specs=pl.BlockSpec((tq, 2 * MLA_V), lambda b, h, i: (b * nq + i, h)),
        out_shape=jax.ShapeDtypeStruct((batch * seq, MLA_HEADS * MLA_V), BF16),
        compiler_params=_params(3),
        name="mla_attn",
    )(q, k, v)


def _memkv_kernel(mem_ref, g_ref, w_ref, kv_ref):
    kv_ref[...] = _dot(_rms(mem_ref[...], g_ref[...]).astype(BF16), w_ref[...]).astype(BF16)


def _memkv(mem2, g, w, *, batch, n_mem):
    d = mem2.shape[1]
    return pl.pallas_call(
        _memkv_kernel,
        grid=(batch,),
        in_specs=[pl.BlockSpec((n_mem, d), lambda b: (b, 0)), _full(g.shape), _full(w.shape)],
        out_specs=pl.BlockSpec((n_mem, 2 * MEM_W), lambda b: (b, 0)),
        out_shape=jax.ShapeDtypeStruct((batch * n_mem, 2 * MEM_W), BF16),
        compiler_params=_params(1),
        name="memkv",
    )(mem2, g, w)


def _merge_kernel(x_ref, yna_ref, ymla_ref, qmem_ref, kvm_ref, gmix_ref, wg_ref, wona_ref, womla_ref,
                  womem_ref, wout_ref, gffn_ref, wrh_ref, wrl_ref, br_ref, h1_ref, xn_ref, lg_ref):
    x = x_ref[...]
    d = x.shape[1]
    n = _rms(x, gmix_ref[...]).astype(BF16)

    ymem = []
    for h in range(MEM_HEADS):
        cols = slice(h * LANES, (h + 1) * LANES)
        lg = _dot_nt(qmem_ref[:, cols], kvm_ref[:, cols])
        p = jnp.exp(lg - jnp.max(lg, axis=1, keepdims=True))
        l = jnp.sum(p, axis=1, keepdims=True)
        vh = kvm_ref[:, MEM_W + h * LANES:MEM_W + (h + 1) * LANES]
        ymem.append((_dot(p.astype(BF16), vh) * (1.0 / l)).astype(BF16))
    ymem = jnp.concatenate(ymem, axis=1)

    merged = None
    for j, (y, w_ref) in enumerate(((yna_ref[...], wona_ref), (ymla_ref[...], womla_ref), (ymem, womem_ref))):
        gate = _sigmoid(_dot(n, wg_ref[:, j * d:(j + 1) * d]))
        term = gate * _dot(y, w_ref[...])
        merged = term if merged is None else merged + term
    h1 = x + _dot(merged.astype(BF16), wout_ref[...])
    h1_ref[...] = h1

    xn = _rms(h1, gffn_ref[...])
    hi = lax.bitcast_convert_type(lax.bitcast_convert_type(xn, jnp.int32) & jnp.int32(-65536), F32)
    xhi = hi.astype(BF16)
    xlo = (xn - hi).astype(BF16)
    lg_ref[...] = (_dot(xhi, wrh_ref[...]) + (_dot(xhi, wrl_ref[...]) + _dot(xlo, wrh_ref[...]))
                   + br_ref[...])
    xn_ref[...] = xn


def _merge(x2, yna, ymla, qmem, kvm, gmix, wg, wona, womla, womem, wout, gffn, wrh, wrl, br, *, seq, n_mem, tm):
    t, d = x2.shape
    nblk = seq // tm
    row = lambda w: pl.BlockSpec((tm, w), lambda i: (i, 0))
    return pl.pallas_call(
        _merge_kernel,
        grid=(t // tm,),
        in_specs=[row(d), row(NA_W), row(MLA_HEADS * MLA_V), row(MEM_W),
                  pl.BlockSpec((n_mem, 2 * MEM_W), lambda i: (i // nblk, 0)),
                  _full(gmix.shape), _full(wg.shape), _full(wona.shape), _full(womla.shape),
                  _full(womem.shape), _full(wout.shape), _full(gffn.shape), _full(wrh.shape),
                  _full(wrl.shape), _full(br.shape)],
        out_specs=[row(d), row(d), row(LANES)],
        out_shape=[jax.ShapeDtypeStruct((t, d), F32),
                   jax.ShapeDtypeStruct((t, d), F32),
                   jax.ShapeDtypeStruct((t, LANES), F32)],
        compiler_params=_params(1),
        name="merge",
    )(x2, yna, ymla, qmem, kvm, gmix, wg, wona, womla, womem, wout, gffn, wrh, wrl, br)


def _route_kernel(lg_ref, idx_ref, aff_ref, bnd_ref, cs_ref, *, cap, nsplit, tb):
    s = lg_ref.shape[0]
    lg = lg_ref[...]
    ex = jnp.exp(lg - jnp.max(lg, axis=1, keepdims=True))
    aff = ex / jnp.sum(ex, axis=1, keepdims=True)
    aff_ref[...] = aff

    aff_t = aff.T[:N_EXPERTS, :]

    def search(i, lo):
        cand = lo | lax.shift_left(jnp.int32(1), 30 - i)
        cnt = jnp.sum((aff_t >= lax.bitcast_convert_type(cand, F32)).astype(jnp.int32), axis=1, keepdims=True)
        return jnp.where(cnt >= cap, cand, lo)

    thr_col = lax.bitcast_convert_type(
        lax.fori_loop(0, 31, search, jnp.zeros((N_EXPERTS, 1), jnp.int32)), F32)
    thr_sq = jnp.concatenate([jnp.broadcast_to(thr_col, (N_EXPERTS, LANES)),
                              jnp.zeros((LANES - N_EXPERTS, LANES), F32)], axis=0)
    thr = thr_sq.T[0:1, :]
    gt = aff > thr
    eq = aff == thr
    need = cap - jnp.sum(gt.astype(jnp.int32), axis=0, keepdims=True)

    tri = (lax.broadcasted_iota(jnp.int32, (tb, tb), 0)
           >= lax.broadcasted_iota(jnp.int32, (tb, tb), 1)).astype(BF16)

    def cumsum_blocks(mask_of_block):
        carry = jnp.zeros((1, LANES), F32)
        for blk in range(s // tb):
            rows = slice(blk * tb, (blk + 1) * tb)
            c = _dot(tri, mask_of_block(rows).astype(BF16)) + carry
            cs_ref[rows, :] = c
            carry = c[tb - 1:tb, :]

    cumsum_blocks(lambda rows: eq[rows, :])
    eq_rank = cs_ref[...] - eq.astype(F32)
    sel = gt | (eq & (eq_rank < need.astype(F32)))
    cumsum_blocks(lambda rows: sel[rows, :])

    sq = s // nsplit
    sub = lax.broadcasted_iota(jnp.int32, (8, LANES), 0)
    bnd = jnp.zeros((8, LANES), jnp.int32)
    for k in range(1, nsplit + 1):
        bnd = jnp.where(sub == k, cs_ref[k * sq - 1:k * sq, :].astype(jnp.int32), bnd)
    bnd_ref[0] = bnd

    jrow = lax.broadcasted_iota(jnp.int32, (1, cap), 1).astype(F32)
    ones = jnp.ones((8, tb), BF16)
    for e in range(N_EXPERTS):
        cnt = jnp.zeros((8, cap), F32)
        for blk in range(s // tb):
            col = cs_ref[blk * tb:(blk + 1) * tb, e:e + 1]
            cnt = cnt + _dot(ones, jnp.where(col <= jrow, 1.0, 0.0).astype(BF16))
        idx_ref[0, e:e + 1, :] = cnt[0:1, :].astype(jnp.int32)


def _route(lg, *, batch, seq, cap, nsplit):
    tb = min(512, seq)
    return pl.pallas_call(
        functools.partial(_route_kernel, cap=cap, nsplit=nsplit, tb=tb),
        grid=(batch,),
        in_specs=[pl.BlockSpec((seq, LANES), lambda b: (b, 0))],
        out_specs=[pl.BlockSpec((1, N_EXPERTS, cap), lambda b: (b, 0, 0)),
                   pl.BlockSpec((seq, LANES), lambda b: (b, 0)),
                   pl.BlockSpec((1, 8, LANES), lambda b: (b, 0, 0))],
        out_shape=[jax.ShapeDtypeStruct((batch, N_EXPERTS, cap), jnp.int32),
                   jax.ShapeDtypeStruct((batch * seq, LANES), F32),
                   jax.ShapeDtypeStruct((batch, 8, LANES), jnp.int32)],
        scratch_shapes=[pltpu.VMEM((seq, LANES), F32)],
        compiler_params=_params(1),
        name="route",
    )(lg)


def _expert_kernel(idx_ref, xn_ref, aff_ref, wg_ref, wu_ref, wd_ref, yg_ref, xs_ref, ga_ref, *, cap):
    e = pl.program_id(1)

    def gather(i, c):
        t = idx_ref[0, 0, i]
        xs_ref[pl.ds(i, 1), :] = xn_ref[pl.ds(t, 1), :]
        ga_ref[pl.ds(i, 1), :] = aff_ref[pl.ds(t, 1), :]
        return c

    lax.fori_loop(0, cap, gather, 0, unroll=8)

    xs = xs_ref[...].astype(BF16)
    hg = _dot(xs, wg_ref[0])
    hu = _dot(xs, wu_ref[0])
    act = (hg * _sigmoid(hg) * hu).astype(BF16)
    y = _dot(act, wd_ref[0])
    lane = lax.broadcasted_iota(jnp.int32, ga_ref.shape, 1)
    g = jnp.sum(jnp.where(lane == e, ga_ref[...], 0.0), axis=1, keepdims=True)
    yg_ref[0] = y * g


def _experts(idx3, xn, aff, wg, wu, wd, *, batch, seq, cap):
    d = wg.shape[1]
    ff = wg.shape[2]
    ne = N_EXPERTS
    once = pl.Buffered(1)
    return pl.pallas_call(
        functools.partial(_expert_kernel, cap=cap),
        grid=(batch, ne),
        in_specs=[pl.BlockSpec((1, 1, cap), lambda b, e: (b * ne + e, 0, 0), memory_space=pltpu.SMEM),
                  pl.BlockSpec((seq, d), lambda b, e: (b, 0), pipeline_mode=once),
                  pl.BlockSpec((seq, LANES), lambda b, e: (b, 0), pipeline_mode=once),
                  pl.BlockSpec((1, d, ff), lambda b, e: (e, 0, 0)),
                  pl.BlockSpec((1, d, ff), lambda b, e: (e, 0, 0)),
                  pl.BlockSpec((1, ff, d), lambda b, e: (e, 0, 0))],
        out_specs=pl.BlockSpec((1, cap, d), lambda b, e: (b * ne + e, 0, 0)),
        out_shape=jax.ShapeDtypeStruct((batch * ne, cap, d), F32),
        scratch_shapes=[pltpu.VMEM((cap, d), F32), pltpu.VMEM((cap, LANES), F32)],
        compiler_params=_params(2),
        name="experts",
    )(idx3, xn, aff, wg, wu, wd)


def _combine_kernel(idx_ref, bnd_ref, h1_ref, yg_ref, gfin_ref, out_ref, *, sq):
    q = pl.program_id(1)
    e = pl.program_id(2)

    @pl.when(e == 0)
    def _():
        out_ref[...] = h1_ref[...]

    base = q * sq

    def scatter(i, c):
        t = idx_ref[0, 0, i] - base
        out_ref[pl.ds(t, 1), :] += yg_ref[0, pl.ds(i, 1), :]
        return c

    def scatter4(k, c):
        i = lo + 4 * k
        ts = [idx_ref[0, 0, i + u] - base for u in range(4)]
        new = [out_ref[pl.ds(ts[u], 1), :] + yg_ref[0, pl.ds(i + u, 1), :] for u in range(4)]
        for u in range(4):
            out_ref[pl.ds(ts[u], 1), :] = new[u]
        return c

    lo = bnd_ref[0, q, e]
    hi = bnd_ref[0, q + 1, e]
    quads = lax.shift_right_logical(hi - lo, 2)
    lax.fori_loop(0, quads, scatter4, 0)
    lax.fori_loop(lo + 4 * quads, hi, scatter, 0)

    @pl.when(e == pl.num_programs(2) - 1)
    def _():
        out_ref[...] = _rms(out_ref[...], gfin_ref[...])


def _combine(idx3, bnd, h1, yg, gfin, *, batch, seq, cap, nsplit):
    d = h1.shape[1]
    ne = N_EXPERTS
    sq = seq // nsplit
    return pl.pallas_call(
        functools.partial(_combine_kernel, sq=sq),
        grid=(batch, nsplit, ne),
        in_specs=[pl.BlockSpec((1, 1, cap), lambda b, q, e: (b * ne + e, 0, 0), memory_space=pltpu.SMEM),
                  pl.BlockSpec((1, 8, LANES), lambda b, q, e: (b, 0, 0), memory_space=pltpu.SMEM),
                  pl.BlockSpec((sq, d), lambda b, q, e: (b * nsplit + q, 0)),
                  pl.BlockSpec((1, cap, d), lambda b, q, e: (b * ne + e, 0, 0)),
                  _full(gfin.shape)],
        out_specs=pl.BlockSpec((sq, d), lambda b, q, e: (b * nsplit + q, 0)),
        out_shape=jax.ShapeDtypeStruct((batch * seq, d), F32),
        compiler_params=_params(3),
        name="combine",
    )(idx3, bnd, h1, yg, gfin)


def _rope_tables(seq):
    t = jnp.arange(seq)
    half = MLA_ROPE // 2
    inv = ROPE_BASE ** (-jnp.arange(0, half, 2, dtype=F32) / half)
    ang_r = (t // GRID_W).astype(F32)[:, None] * inv[None, :]
    ang_c = (t % GRID_W).astype(F32)[:, None] * inv[None, :]
    cos = jnp.concatenate([jnp.cos(ang_r), jnp.cos(ang_r), jnp.cos(ang_c), jnp.cos(ang_c)], axis=1)
    sin = jnp.concatenate([-jnp.sin(ang_r), jnp.sin(ang_r), -jnp.sin(ang_c), jnp.sin(ang_c)], axis=1)
    pad = LANES - MLA_NOPE - MLA_ROPE
    cos = jnp.concatenate([jnp.ones((seq, MLA_NOPE), F32), cos, jnp.zeros((seq, pad), F32)], axis=1)
    sin = jnp.concatenate([jnp.zeros((seq, MLA_NOPE), F32), sin, jnp.zeros((seq, pad), F32)], axis=1)
    return cos, sin


_q = MLA_ROPE // 4
ROPE_SWAP = np.concatenate([np.arange(_q, 2 * _q), np.arange(0, _q),
                            np.arange(3 * _q, 4 * _q), np.arange(2 * _q, 3 * _q)])


def kernel(x, mem, g_mix, w_in, na_rpb, mla_q_norm, w_mla_uq, mla_kv_norm, w_mla_ukv, g_mem, w_mem_kv,
           w_o_na, w_o_mla, w_o_mem, w_out, g_ffn, w_router, b_router, w_gate, w_up, w_down, g_final):
    batch, seq, d = x.shape
    n_mem = mem.shape[1]
    depth = g_mix.shape[0]
    q_rank = mla_q_norm.shape[1]
    kv_rank = mla_kv_norm.shape[1]
    cap = EC_FACTOR * seq // N_EXPERTS
    nsplit = 2
    tm = 512
    pad = LANES - MLA_NOPE - MLA_ROPE
    cos, sin = _rope_tables(seq)

    assert depth == 1
    h = x.reshape(batch * seq, d)
    for l in range(depth):
        w = w_in[l]
        o = np.cumsum([0, NA_W, NA_W, NA_W, q_rank, kv_rank, MLA_ROPE, MEM_W, 3 * d])
        wna = w[:, o[0]:o[3]].astype(BF16)
        wcq = w[:, o[3]:o[4]].astype(BF16)
        wckv = w[:, o[4]:o[5]].astype(BF16)
        wkr_raw = w[:, o[5]:o[6]]
        wqm = w[:, o[6]:o[7]].astype(BF16)
        wgate_cols = w[:, o[7]:o[8]].astype(BF16)
        z = lambda *s: jnp.zeros(s, F32)
        wkr = jnp.concatenate([z(d, MLA_NOPE), wkr_raw, z(d, pad),
                               z(d, MLA_NOPE), wkr_raw[:, ROPE_SWAP], z(d, pad)], axis=1).astype(BF16)
        uq = w_mla_uq[l].reshape(q_rank, MLA_HEADS, MLA_NOPE + MLA_ROPE)
        uq_a = jnp.concatenate([uq, z(q_rank, MLA_HEADS, pad)], axis=2)
        uq_b = jnp.concatenate([z(q_rank, MLA_HEADS, MLA_NOPE), uq[:, :, MLA_NOPE:][:, :, ROPE_SWAP],
                                z(q_rank, MLA_HEADS, pad)], axis=2)
        wuq = jnp.concatenate([uq_a.reshape(q_rank, -1), uq_b.reshape(q_rank, -1)], axis=1).astype(BF16)
        ukv = w_mla_ukv[l].reshape(kv_rank, MLA_HEADS, MLA_NOPE + MLA_V)
        uk = jnp.concatenate([ukv[:, :, :MLA_NOPE], z(kv_rank, MLA_HEADS, LANES - MLA_NOPE)], axis=2)
        uv = jnp.concatenate([ukv[:, :, MLA_NOPE:], z(kv_rank, MLA_HEADS, LANES - MLA_V)], axis=2)
        wukv = jnp.concatenate([uk.reshape(kv_rank, -1), uv.reshape(kv_rank, -1)], axis=1).astype(BF16)
        wr = jnp.concatenate([w_router[l].astype(F32), z(d, LANES - N_EXPERTS)], axis=1)
        wr_hi = lax.reduce_precision(wr, exponent_bits=8, mantissa_bits=7)
        wrh = wr_hi.astype(BF16)
        wrl = (wr - wr_hi).astype(BF16)
        br = jnp.concatenate([b_router[l].astype(F32), jnp.full((LANES - N_EXPERTS,), NEG, F32)])[None, :]

        na, qmla, kmla, vmla, qmem = _inproj(
            h, g_mix[l][None, :], wna, wcq, wckv, wkr, wqm, mla_q_norm[l][None, :], mla_kv_norm[l][None, :],
            wuq, wukv, cos, sin, seq=seq, tm=tm)
        y_na = _na_attention(na, _na_bias_table(na_rpb[l]), batch=batch, seq=seq)
        y_mla = _mla_attention(qmla, kmla, vmla, batch=batch, seq=seq, tq=256)
        kvm = _memkv(mem.reshape(batch * n_mem, d), g_mem[l][None, :], w_mem_kv[l].astype(BF16),
                     batch=batch, n_mem=n_mem)
        h1, xn, lg = _merge(h, y_na, y_mla, qmem, kvm, g_mix[l][None, :], wgate_cols,
                             w_o_na[l].astype(BF16), w_o_mla[l].astype(BF16), w_o_mem[l].astype(BF16),
                             w_out[l].astype(BF16), g_ffn[l][None, :], wrh, wrl, br,
                             seq=seq, n_mem=n_mem, tm=256)
        idx, aff, bnd = _route(lg, batch=batch, seq=seq, cap=cap, nsplit=nsplit)
        idx3 = idx.reshape(batch * N_EXPERTS, 1, cap)
        yg = _experts(idx3, xn, aff, w_gate[l].astype(BF16), w_up[l].astype(BF16), w_down[l].astype(BF16),
                      batch=batch, seq=seq, cap=cap)
        h = _combine(idx3, bnd, h1, yg, g_final[None, :], batch=batch, seq=seq, cap=cap, nsplit=nsplit)
    return h.reshape(batch, seq, d)
```

```python
import functools

import jax
import jax.numpy as jnp
import numpy as np
from jax import lax
from jax.experimental import pallas as pl
from jax.experimental.pallas import tpu as pltpu

GRID_W = 64
EPS = 1e-6
NA_HEADS = 8
NA_HEAD_DIM = 64
NA_WIN_R = 8
NA_WIN_C = 16
MLA_HEADS = 8
MLA_NOPE = 64
MLA_ROPE = 32
MLA_V = 64
ROPE_BASE = 10000.0
MEM_HEADS = 4
MEM_HEAD_DIM = 128
N_EXPERTS = 16
EC_FACTOR = 2
NA_W = NA_HEADS * NA_HEAD_DIM
MEM_W = MEM_HEADS * MEM_HEAD_DIM
LANES = 128
NEG = -1e30
VMEM_LIMIT = 56 * 1024 * 1024

F32 = jnp.float32
BF16 = jnp.bfloat16


def _dot(a, b):
    return jnp.dot(a, b, preferred_element_type=F32)


def _dot_nt(a, b):
    return lax.dot_general(a, b, (((1,), (1,)), ((), ())), preferred_element_type=F32)


def _rms(xf, g):
    return xf * lax.rsqrt(jnp.mean(xf * xf, axis=-1, keepdims=True) + EPS) * g


def _sigmoid(v):
    return 1.0 / (1.0 + jnp.exp(-v))


def _full(shape):
    return pl.BlockSpec(shape, lambda *_: (0,) * len(shape))


def _params(n_axes):
    return pltpu.CompilerParams(dimension_semantics=("arbitrary",) * n_axes,
                                vmem_limit_bytes=VMEM_LIMIT)


def _inproj_kernel(x_ref, gmix_ref, wna_ref, wcq_ref, wckv_ref, wkr_ref, wqm_ref, qn_ref, kvn_ref,
                   wuq_ref, wukv_ref, cos_ref, sin_ref,
                   na_ref, qmla_ref, kmla_ref, vmla_ref, qmem_ref, *, mla_scale, mem_scale):
    n = _rms(x_ref[...], gmix_ref[...]).astype(BF16)
    z_na = _dot(n, wna_ref[...])
    na_ref[:, :NA_W] = (z_na[:, :NA_W] * (NA_HEAD_DIM ** -0.5)).astype(BF16)
    na_ref[:, NA_W:] = z_na[:, NA_W:].astype(BF16)
    qmem_ref[...] = (_dot(n, wqm_ref[...]) * mem_scale).astype(BF16)

    cos = cos_ref[...]
    sin = sin_ref[...]
    zkr = _dot(n, wkr_ref[...])
    kpe = zkr[:, :LANES] * cos + zkr[:, LANES:] * sin

    cqn = _rms(_dot(n, wcq_ref[...]), qn_ref[...]).astype(BF16)
    zq = _dot(cqn, wuq_ref[...])
    hw = MLA_HEADS * LANES
    for h in range(MLA_HEADS):
        a = zq[:, h * LANES:(h + 1) * LANES]
        b = zq[:, hw + h * LANES:hw + (h + 1) * LANES]
        qmla_ref[:, h * LANES:(h + 1) * LANES] = ((a * cos + b * sin) * mla_scale).astype(BF16)

    ckvn = _rms(_dot(n, wckv_ref[...]), kvn_ref[...]).astype(BF16)
    zkv = _dot(ckvn, wukv_ref[...])
    for h in range(MLA_HEADS):
        kmla_ref[:, h * LANES:(h + 1) * LANES] = (zkv[:, h * LANES:(h + 1) * LANES] + kpe).astype(BF16)
    is_v = lax.broadcasted_iota(jnp.int32, (zkv.shape[0], LANES), 1) < MLA_V
    for h in range(MLA_HEADS):
        blk = zkv[:, hw + h * LANES:hw + (h + 1) * LANES]
        vmla_ref[:, h * LANES:(h + 1) * LANES] = jnp.where(is_v, blk, 1.0).astype(BF16)


def _inproj(x2, gmix, wna, wcq, wckv, wkr, wqm, qn, kvn, wuq, wukv, cos, sin, *, seq, tm):
    t, d = x2.shape
    nblk = seq // tm
    row = lambda w: pl.BlockSpec((tm, w), lambda i: (i, 0))
    rope = pl.BlockSpec((tm, LANES), lambda i: (i % nblk, 0))
    hw = MLA_HEADS * LANES
    kern = functools.partial(_inproj_kernel, mla_scale=(MLA_NOPE + MLA_ROPE) ** -0.5,
                             mem_scale=MEM_HEAD_DIM ** -0.5)
    return pl.pallas_call(
        kern,
        grid=(t // tm,),
        in_specs=[row(d), _full(gmix.shape), _full(wna.shape), _full(wcq.shape), _full(wckv.shape),
                  _full(wkr.shape), _full(wqm.shape), _full(qn.shape), _full(kvn.shape),
                  _full(wuq.shape), _full(wukv.shape), rope, rope],
        out_specs=[row(3 * NA_W), row(hw), row(hw), row(hw), row(MEM_W)],
        out_shape=[jax.ShapeDtypeStruct((t, 3 * NA_W), BF16),
                   jax.ShapeDtypeStruct((t, hw), BF16),
                   jax.ShapeDtypeStruct((t, hw), BF16),
                   jax.ShapeDtypeStruct((t, hw), BF16),
                   jax.ShapeDtypeStruct((t, MEM_W), BF16)],
        compiler_params=_params(1),
        name="inproj",
    )(x2, gmix, wna, wcq, wckv, wkr, wqm, qn, kvn, wuq, wukv, cos, sin)


NA_QROWS = NA_WIN_R // 2
NA_BAND = 12


def _na_kernel(q_ref, k_ref, v_ref, bias_ref, y_ref, *, rows):
    i = pl.program_id(1)
    us = jnp.clip(i * NA_QROWS - NA_WIN_R // 2, 0, rows - NA_BAND)
    start = pl.multiple_of(us * GRID_W, GRID_W)
    band = NA_BAND * GRID_W
    first = lax.broadcasted_iota(jnp.int32, (NA_QROWS * GRID_W, LANES), 1) < NA_HEAD_DIM
    kfirst = lax.broadcasted_iota(jnp.int32, (band, LANES), 1) < NA_HEAD_DIM
    for hp in range(NA_HEADS // 2):
        cols = slice(hp * LANES, (hp + 1) * LANES)
        qp = q_ref[:, cols]
        kp = k_ref[pl.ds(start, band), cols]
        vp = v_ref[pl.ds(start, band), cols]
        outs = []
        for s in range(2):
            mine = first if s == 0 else ~first
            qm = jnp.where(mine, qp, jnp.zeros_like(qp))
            lg = _dot_nt(qm, kp) + bias_ref[0, 2 * hp + s]
            p = jnp.exp((lg - jnp.max(lg, axis=1, keepdims=True)).astype(BF16))
            oa = _dot(p, jnp.where(kfirst if s == 0 else ~kfirst, vp, jnp.ones_like(vp)))
            other = (1 - s) * NA_HEAD_DIM
            outs.append(oa * (1.0 / oa[:, other:other + 1]))
        y_ref[:, cols] = jnp.where(first, outs[0], outs[1]).astype(BF16)


def _na_bias_table(rpb):
    q, bn, wr, wc = NA_QROWS, NA_BAND, NA_WIN_R, NA_WIN_C
    rho = np.arange(q)
    own = np.stack([0 * rho, rho, 0 * rho + (bn - wr)])
    dd = np.stack([rho, 0 * rho + wr // 2, rho + wr // 2])
    ip = np.arange(bn)[None, None, :] - own[:, :, None]
    rvalid = (ip >= 0) & (ip < wr)
    dr = ip - dd[:, :, None] + (wr - 1)
    rsel = (dr[..., None] == np.arange(2 * wr - 1)) & rvalid[..., None]
    c = np.arange(GRID_W)[:, None]
    j = np.arange(GRID_W)[None, :]
    cstart = np.clip(c - wc // 2, 0, GRID_W - wc)
    cvalid = (j >= cstart) & (j < cstart + wc)
    csel = ((j - c + wc - 1)[None] == np.arange(2 * wc - 1)[:, None, None]) & cvalid[None]
    tab = jnp.einsum("hrk,vpir,kcj->vhpcij", rpb.astype(F32), jnp.asarray(rsel, F32), jnp.asarray(csel, F32),
                     precision=lax.Precision.HIGHEST)
    valid = rvalid[:, None, :, None, :, None] & cvalid[None, None, None, :, None, :]
    tab = jnp.where(jnp.asarray(valid), tab, NEG)
    return tab.reshape(3, NA_HEADS, q * GRID_W, bn * GRID_W)


def _na_attention(na, bias, *, batch, seq):
    rows = seq // GRID_W
    nblk = rows // NA_QROWS
    assert rows % NA_QROWS == 0 and rows >= NA_BAND and NA_BAND >= NA_QROWS + NA_WIN_R - 1
    nq = NA_QROWS * GRID_W

    def bias_map(b, i):
        return (jnp.where(i == 0, 0, jnp.where(i == nblk - 1, 2, 1)), 0, 0, 0)

    return pl.pallas_call(
        functools.partial(_na_kernel, rows=rows),
        grid=(batch, nblk),
        in_specs=[pl.BlockSpec((nq, NA_W), lambda b, i: (b * nblk + i, 0)),
                  pl.BlockSpec((seq, NA_W), lambda b, i: (b, 1)),
                  pl.BlockSpec((seq, NA_W), lambda b, i: (b, 2)),
                  pl.BlockSpec((1, NA_HEADS, nq, NA_BAND * GRID_W), bias_map)],
        out_specs=pl.BlockSpec((nq, NA_W), lambda b, i: (b * nblk + i, 0)),
        out_shape=jax.ShapeDtypeStruct((batch * seq, NA_W), BF16),
        compiler_params=_params(2),
        name="na_attn",
    )(na, na, na, bias)


def _mla_kernel(q_ref, k_ref, v_ref, o_ref):
    tq = q_ref.shape[0]
    first = lax.broadcasted_iota(jnp.int32, (tq, LANES), 1) < MLA_V
    for pr in range(MLA_HEADS_PER_STEP // 2):
        outs = []
        for s in range(2):
            cols = slice((2 * pr + s) * LANES, (2 * pr + s + 1) * LANES)
            lg = _dot_nt(q_ref[:, cols], k_ref[:, cols])
            p = jnp.exp((lg - jnp.max(lg, axis=1, keepdims=True)).astype(BF16))
            oa = _dot(p, v_ref[:, cols])
            outs.append(oa * (1.0 / oa[:, MLA_V:MLA_V + 1]))
        o_ref[:, pr * LANES:(pr + 1) * LANES] = jnp.where(
            first, outs[0], pltpu.roll(outs[1], MLA_V, 1)).astype(BF16)


MLA_HEADS_PER_STEP = 4


def _mla_attention(q, k, v, *, batch, seq, tq):
    nq = seq // tq
    hps = MLA_HEADS_PER_STEP
    hp = MLA_HEADS // hps
    return pl.pallas_call(
        _mla_kernel,
        grid=(batch, hp, nq),
        in_specs=[pl.BlockSpec((tq, hps * LANES), lambda b, h, i: (b * nq + i, h)),
                  pl.BlockSpec((seq, hps * LANES), lambda b, h, i: (b, h)),
                  pl.BlockSpec((seq, hps * LANES), lambda b, h, i: (b, h))],
        out_specs=pl.BlockSpec((tq, hps * MLA_V), lambda b, h, i: (b * nq + i, h)),
        out_shape=jax.ShapeDtypeStruct((batch * seq, MLA_HEADS * MLA_V), BF16),
        compiler_params=_params(3),
        name="mla_attn",
    )(q, k, v)


def _memkv_kernel(mem_ref, g_ref, w_ref, kv_ref):
    kv_ref[...] = _dot(_rms(mem_ref[...], g_ref[...]).astype(BF16), w_ref[...]).astype(BF16)


def _memkv(mem2, g, w, *, batch, n_mem):
    d = mem2.shape[1]
    return pl.pallas_call(
        _memkv_kernel,
        grid=(batch,),
        in_specs=[pl.BlockSpec((n_mem, d), lambda b: (b, 0)), _full(g.shape), _full(w.shape)],
        out_specs=pl.BlockSpec((n_mem, 2 * MEM_W), lambda b: (b, 0)),
        out_shape=jax.ShapeDtypeStruct((batch * n_mem, 2 * MEM_W), BF16),
        compiler_params=_params(1),
        name="memkv",
    )(mem2, g, w)


def _merge_kernel(x_ref, yna_ref, ymla_ref, qmem_ref, kvm_ref, gmix_ref, wg_ref, wona_ref, womla_ref,
                  womem_ref, wout_ref, gffn_ref, wrh_ref, wrl_ref, br_ref, h1_ref, xn_ref, lg_ref):
    x = x_ref[...]
    d = x.shape[1]
    n = _rms(x, gmix_ref[...]).astype(BF16)

    ymem = []
    for h in range(MEM_HEADS):
        cols = slice(h * LANES, (h + 1) * LANES)
        lg = _dot_nt(qmem_ref[:, cols], kvm_ref[:, cols])
        p = jnp.exp(lg - jnp.max(lg, axis=1, keepdims=True))
        l = jnp.sum(p, axis=1, keepdims=True)
        vh = kvm_ref[:, MEM_W + h * LANES:MEM_W + (h + 1) * LANES]
        ymem.append((_dot(p.astype(BF16), vh) * (1.0 / l)).astype(BF16))
    ymem = jnp.concatenate(ymem, axis=1)

    merged = None
    for j, (y, w_ref) in enumerate(((yna_ref[...], wona_ref), (ymla_ref[...], womla_ref), (ymem, womem_ref))):
        gate = _sigmoid(_dot(n, wg_ref[:, j * d:(j + 1) * d]))
        term = gate * _dot(y, w_ref[...])
        merged = term if merged is None else merged + term
    h1 = x + _dot(merged.astype(BF16), wout_ref[...])
    h1_ref[...] = h1

    xn = _rms(h1, gffn_ref[...])
    hi = lax.bitcast_convert_type(lax.bitcast_convert_type(xn, jnp.int32) & jnp.int32(-65536), F32)
    xhi = hi.astype(BF16)
    xlo = (xn - hi).astype(BF16)
    wsum_lo = _dot(xhi, wrl_ref[...]) + _dot(xlo, wrh_ref[...]) + _dot(xlo, wrl_ref[...])
    lg_ref[...] = _dot(xhi, wrh_ref[...]) + wsum_lo + br_ref[...]
    xn_ref[...] = xn


def _merge(x2, yna, ymla, qmem, kvm, gmix, wg, wona, womla, womem, wout, gffn, wrh, wrl, br, *, seq, n_mem, tm):
    t, d = x2.shape
    nblk = seq // tm
    row = lambda w: pl.BlockSpec((tm, w), lambda i: (i, 0))
    return pl.pallas_call(
        _merge_kernel,
        grid=(t // tm,),
        in_specs=[row(d), row(NA_W), row(MLA_HEADS * MLA_V), row(MEM_W),
                  pl.BlockSpec((n_mem, 2 * MEM_W), lambda i: (i // nblk, 0)),
                  _full(gmix.shape), _full(wg.shape), _full(wona.shape), _full(womla.shape),
                  _full(womem.shape), _full(wout.shape), _full(gffn.shape), _full(wrh.shape),
                  _full(wrl.shape), _full(br.shape)],
        out_specs=[row(d), row(d), row(LANES)],
        out_shape=[jax.ShapeDtypeStruct((t, d), F32),
                   jax.ShapeDtypeStruct((t, d), F32),
                   jax.ShapeDtypeStruct((t, LANES), F32)],
        compiler_params=_params(1),
        name="merge",
    )(x2, yna, ymla, qmem, kvm, gmix, wg, wona, womla, womem, wout, gffn, wrh, wrl, br)


def _route_kernel(lg_ref, idx_ref, aff_ref, bnd_ref, cs_ref, *, cap, nsplit, tb):
    s = lg_ref.shape[0]
    lg = lg_ref[...]
    ex = jnp.exp(lg - jnp.max(lg, axis=1, keepdims=True))
    aff = ex / jnp.sum(ex, axis=1, keepdims=True)
    aff_ref[...] = aff

    aff_t = aff.T[:N_EXPERTS, :]

    def search(i, lo):
        cand = lo | lax.shift_left(jnp.int32(1), 30 - i)
        cnt = jnp.sum((aff_t >= lax.bitcast_convert_type(cand, F32)).astype(jnp.int32), axis=1, keepdims=True)
        return jnp.where(cnt >= cap, cand, lo)

    thr_col = lax.bitcast_convert_type(
        lax.fori_loop(0, 31, search, jnp.zeros((N_EXPERTS, 1), jnp.int32)), F32)
    thr_sq = jnp.concatenate([jnp.broadcast_to(thr_col, (N_EXPERTS, LANES)),
                              jnp.zeros((LANES - N_EXPERTS, LANES), F32)], axis=0)
    thr = thr_sq.T[0:1, :]
    gt = aff > thr
    eq = aff == thr
    need = cap - jnp.sum(gt.astype(jnp.int32), axis=0, keepdims=True)

    tri = (lax.broadcasted_iota(jnp.int32, (tb, tb), 0)
           >= lax.broadcasted_iota(jnp.int32, (tb, tb), 1)).astype(BF16)

    def cumsum_blocks(mask_of_block):
        carry = jnp.zeros((1, LANES), F32)
        for blk in range(s // tb):
            rows = slice(blk * tb, (blk + 1) * tb)
            c = _dot(tri, mask_of_block(rows).astype(BF16)) + carry
            cs_ref[rows, :] = c
            carry = c[tb - 1:tb, :]

    cumsum_blocks(lambda rows: eq[rows, :])
    eq_rank = cs_ref[...] - eq.astype(F32)
    sel = gt | (eq & (eq_rank < need.astype(F32)))
    cumsum_blocks(lambda rows: sel[rows, :])

    sq = s // nsplit
    sub = lax.broadcasted_iota(jnp.int32, (8, LANES), 0)
    bnd = jnp.zeros((8, LANES), jnp.int32)
    for k in range(1, nsplit + 1):
        bnd = jnp.where(sub == k, cs_ref[k * sq - 1:k * sq, :].astype(jnp.int32), bnd)
    bnd_ref[0] = bnd

    jrow = lax.broadcasted_iota(jnp.int32, (1, cap), 1).astype(F32)
    ones = jnp.ones((8, tb), BF16)
    for e in range(N_EXPERTS):
        cnt = jnp.zeros((8, cap), F32)
        for blk in range(s // tb):
            col = cs_ref[blk * tb:(blk + 1) * tb, e:e + 1]
            cnt = cnt + _dot(ones, jnp.where(col <= jrow, 1.0, 0.0).astype(BF16))
        idx_ref[0, e:e + 1, :] = cnt[0:1, :].astype(jnp.int32)


def _route(lg, *, batch, seq, cap, nsplit):
    tb = min(512, seq)
    return pl.pallas_call(
        functools.partial(_route_kernel, cap=cap, nsplit=nsplit, tb=tb),
        grid=(batch,),
        in_specs=[pl.BlockSpec((seq, LANES), lambda b: (b, 0))],
        out_specs=[pl.BlockSpec((1, N_EXPERTS, cap), lambda b: (b, 0, 0)),
                   pl.BlockSpec((seq, LANES), lambda b: (b, 0)),
                   pl.BlockSpec((1, 8, LANES), lambda b: (b, 0, 0))],
        out_shape=[jax.ShapeDtypeStruct((batch, N_EXPERTS, cap), jnp.int32),
                   jax.ShapeDtypeStruct((batch * seq, LANES), F32),
                   jax.ShapeDtypeStruct((batch, 8, LANES), jnp.int32)],
        scratch_shapes=[pltpu.VMEM((seq, LANES), F32)],
        compiler_params=_params(1),
        name="route",
    )(lg)


def _expert_kernel(idx_ref, xn_ref, aff_ref, wg_ref, wu_ref, wd_ref, yg_ref, xs_ref, ga_ref, *, cap):
    e = pl.program_id(1)

    def gather(i, c):
        t = idx_ref[0, 0, i]
        xs_ref[pl.ds(i, 1), :] = xn_ref[pl.ds(t, 1), :]
        ga_ref[pl.ds(i, 1), :] = aff_ref[pl.ds(t, 1), :]
        return c

    lax.fori_loop(0, cap, gather, 0, unroll=8)

    xs = xs_ref[...].astype(BF16)
    hg = _dot(xs, wg_ref[0])
    hu = _dot(xs, wu_ref[0])
    act = (hg * _sigmoid(hg) * hu).astype(BF16)
    y = _dot(act, wd_ref[0])
    lane = lax.broadcasted_iota(jnp.int32, ga_ref.shape, 1)
    g = jnp.sum(jnp.where(lane == e, ga_ref[...], 0.0), axis=1, keepdims=True)
    yg_ref[0] = y * g


def _experts(idx3, xn, aff, wg, wu, wd, *, batch, seq, cap):
    d = wg.shape[1]
    ff = wg.shape[2]
    ne = N_EXPERTS
    once = pl.Buffered(1)
    return pl.pallas_call(
        functools.partial(_expert_kernel, cap=cap),
        grid=(batch, ne),
        in_specs=[pl.BlockSpec((1, 1, cap), lambda b, e: (b * ne + e, 0, 0), memory_space=pltpu.SMEM),
                  pl.BlockSpec((seq, d), lambda b, e: (b, 0), pipeline_mode=once),
                  pl.BlockSpec((seq, LANES), lambda b, e: (b, 0), pipeline_mode=once),
                  pl.BlockSpec((1, d, ff), lambda b, e: (e, 0, 0)),
                  pl.BlockSpec((1, d, ff), lambda b, e: (e, 0, 0)),
                  pl.BlockSpec((1, ff, d), lambda b, e: (e, 0, 0))],
        out_specs=pl.BlockSpec((1, cap, d), lambda b, e: (b * ne + e, 0, 0)),
        out_shape=jax.ShapeDtypeStruct((batch * ne, cap, d), F32),
        scratch_shapes=[pltpu.VMEM((cap, d), F32), pltpu.VMEM((cap, LANES), F32)],
        compiler_params=_params(2),
        name="experts",
    )(idx3, xn, aff, wg, wu, wd)


def _combine_kernel(idx_ref, bnd_ref, h1_ref, yg_ref, gfin_ref, out_ref, *, sq):
    q = pl.program_id(1)
    e = pl.program_id(2)

    @pl.when(e == 0)
    def _():
        out_ref[...] = h1_ref[...]

    base = q * sq

    def scatter(i, c):
        t = idx_ref[0, 0, i] - base
        out_ref[pl.ds(t, 1), :] += yg_ref[0, pl.ds(i, 1), :]
        return c

    def scatter4(k, c):
        i = lo + 4 * k
        ts = [idx_ref[0, 0, i + u] - base for u in range(4)]
        new = [out_ref[pl.ds(ts[u], 1), :] + yg_ref[0, pl.ds(i + u, 1), :] for u in range(4)]
        for u in range(4):
            out_ref[pl.ds(ts[u], 1), :] = new[u]
        return c

    lo = bnd_ref[0, q, e]
    hi = bnd_ref[0, q + 1, e]
    quads = lax.shift_right_logical(hi - lo, 2)
    lax.fori_loop(0, quads, scatter4, 0)
    lax.fori_loop(lo + 4 * quads, hi, scatter, 0)

    @pl.when(e == pl.num_programs(2) - 1)
    def _():
        out_ref[...] = _rms(out_ref[...], gfin_ref[...])


def _combine(idx3, bnd, h1, yg, gfin, *, batch, seq, cap, nsplit):
    d = h1.shape[1]
    ne = N_EXPERTS
    sq = seq // nsplit
    return pl.pallas_call(
        functools.partial(_combine_kernel, sq=sq),
        grid=(batch, nsplit, ne),
        in_specs=[pl.BlockSpec((1, 1, cap), lambda b, q, e: (b * ne + e, 0, 0), memory_space=pltpu.SMEM),
                  pl.BlockSpec((1, 8, LANES), lambda b, q, e: (b, 0, 0), memory_space=pltpu.SMEM),
                  pl.BlockSpec((sq, d), lambda b, q, e: (b * nsplit + q, 0)),
                  pl.BlockSpec((1, cap, d), lambda b, q, e: (b * ne + e, 0, 0)),
                  _full(gfin.shape)],
        out_specs=pl.BlockSpec((sq, d), lambda b, q, e: (b * nsplit + q, 0)),
        out_shape=jax.ShapeDtypeStruct((batch * seq, d), F32),
        compiler_params=_params(3),
        name="combine",
    )(idx3, bnd, h1, yg, gfin)


def _rope_tables(seq):
    t = jnp.arange(seq)
    half = MLA_ROPE // 2
    inv = ROPE_BASE ** (-jnp.arange(0, half, 2, dtype=F32) / half)
    ang_r = (t // GRID_W).astype(F32)[:, None] * inv[None, :]
    ang_c = (t % GRID_W).astype(F32)[:, None] * inv[None, :]
    cos = jnp.concatenate([jnp.cos(ang_r), jnp.cos(ang_r), jnp.cos(ang_c), jnp.cos(ang_c)], axis=1)
    sin = jnp.concatenate([-jnp.sin(ang_r), jnp.sin(ang_r), -jnp.sin(ang_c), jnp.sin(ang_c)], axis=1)
    pad = LANES - MLA_NOPE - MLA_ROPE
    cos = jnp.concatenate([jnp.ones((seq, MLA_NOPE), F32), cos, jnp.zeros((seq, pad), F32)], axis=1)
    sin = jnp.concatenate([jnp.zeros((seq, MLA_NOPE), F32), sin, jnp.zeros((seq, pad), F32)], axis=1)
    return cos, sin


_q = MLA_ROPE // 4
ROPE_SWAP = np.concatenate([np.arange(_q, 2 * _q), np.arange(0, _q),
                            np.arange(3 * _q, 4 * _q), np.arange(2 * _q, 3 * _q)])


def kernel(x, mem, g_mix, w_in, na_rpb, mla_q_norm, w_mla_uq, mla_kv_norm, w_mla_ukv, g_mem, w_mem_kv,
           w_o_na, w_o_mla, w_o_mem, w_out, g_ffn, w_router, b_router, w_gate, w_up, w_down, g_final):
    batch, seq, d = x.shape
    n_mem = mem.shape[1]
    depth = g_mix.shape[0]
    q_rank = mla_q_norm.shape[1]
    kv_rank = mla_kv_norm.shape[1]
    cap = EC_FACTOR * seq // N_EXPERTS
    nsplit = 2
    tm = 512
    pad = LANES - MLA_NOPE - MLA_ROPE
    cos, sin = _rope_tables(seq)

    assert depth == 1
    h = x.reshape(batch * seq, d)
    for l in range(depth):
        w = w_in[l]
        o = np.cumsum([0, NA_W, NA_W, NA_W, q_rank, kv_rank, MLA_ROPE, MEM_W, 3 * d])
        wna = w[:, o[0]:o[3]].astype(BF16)
        wcq = w[:, o[3]:o[4]].astype(BF16)
        wckv = w[:, o[4]:o[5]].astype(BF16)
        wkr_raw = w[:, o[5]:o[6]]
        wqm = w[:, o[6]:o[7]].astype(BF16)
        wgate_cols = w[:, o[7]:o[8]].astype(BF16)
        z = lambda *s: jnp.zeros(s, F32)
        wkr = jnp.concatenate([z(d, MLA_NOPE), wkr_raw, z(d, pad),
                               z(d, MLA_NOPE), wkr_raw[:, ROPE_SWAP], z(d, pad)], axis=1).astype(BF16)
        uq = w_mla_uq[l].reshape(q_rank, MLA_HEADS, MLA_NOPE + MLA_ROPE)
        uq_a = jnp.concatenate([uq, z(q_rank, MLA_HEADS, pad)], axis=2)
        uq_b = jnp.concatenate([z(q_rank, MLA_HEADS, MLA_NOPE), uq[:, :, MLA_NOPE:][:, :, ROPE_SWAP],
                                z(q_rank, MLA_HEADS, pad)], axis=2)
        wuq = jnp.concatenate([uq_a.reshape(q_rank, -1), uq_b.reshape(q_rank, -1)], axis=1).astype(BF16)
        ukv = w_mla_ukv[l].reshape(kv_rank, MLA_HEADS, MLA_NOPE + MLA_V)
        uk = jnp.concatenate([ukv[:, :, :MLA_NOPE], z(kv_rank, MLA_HEADS, LANES - MLA_NOPE)], axis=2)
        uv = jnp.concatenate([ukv[:, :, MLA_NOPE:], z(kv_rank, MLA_HEADS, LANES - MLA_V)], axis=2)
        wukv = jnp.concatenate([uk.reshape(kv_rank, -1), uv.reshape(kv_rank, -1)], axis=1).astype(BF16)
        wr = jnp.concatenate([w_router[l].astype(F32), z(d, LANES - N_EXPERTS)], axis=1)
        wr_hi = lax.bitcast_convert_type(lax.bitcast_convert_type(wr, jnp.int32) & jnp.int32(-65536), F32)
        wrh = wr_hi.astype(BF16)
        wrl = (wr - wr_hi).astype(BF16)
        br = jnp.concatenate([b_router[l].astype(F32), jnp.full((LANES - N_EXPERTS,), NEG, F32)])[None, :]

        na, qmla, kmla, vmla, qmem = _inproj(
            h, g_mix[l][None, :], wna, wcq, wckv, wkr, wqm, mla_q_norm[l][None, :], mla_kv_norm[l][None, :],
            wuq, wukv, cos, sin, seq=seq, tm=tm)
        y_na = _na_attention(na, _na_bias_table(na_rpb[l]), batch=batch, seq=seq)
        y_mla = _mla_attention(qmla, kmla, vmla, batch=batch, seq=seq, tq=256)
        kvm = _memkv(mem.reshape(batch * n_mem, d), g_mem[l][None, :], w_mem_kv[l].astype(BF16),
                     batch=batch, n_mem=n_mem)
        h1, xn, lg = _merge(h, y_na, y_mla, qmem, kvm, g_mix[l][None, :], wgate_cols,
                             w_o_na[l].astype(BF16), w_o_mla[l].astype(BF16), w_o_mem[l].astype(BF16),
                             w_out[l].astype(BF16), g_ffn[l][None, :], wrh, wrl, br,
                             seq=seq, n_mem=n_mem, tm=256)
        idx, aff, bnd = _route(lg, batch=batch, seq=seq, cap=cap, nsplit=nsplit)
        idx3 = idx.reshape(batch * N_EXPERTS, 1, cap)
        yg = _experts(idx3, xn, aff, w_gate[l].astype(BF16), w_up[l].astype(BF16), w_down[l].astype(BF16),
                      batch=batch, seq=seq, cap=cap)
        h = _combine(idx3, bnd, h1, yg, g_final[None, :], batch=batch, seq=seq, cap=cap, nsplit=nsplit)
    return h.reshape(batch, seq, d)
```

```python
import functools

import jax
import jax.numpy as jnp
import numpy as np
from jax import lax
from jax.experimental import pallas as pl
from jax.experimental.pallas import tpu as pltpu

GRID_W = 64
EPS = 1e-6
NA_HEADS = 8
NA_HEAD_DIM = 64
NA_WIN_R = 8
NA_WIN_C = 16
MLA_HEADS = 8
MLA_NOPE = 64
MLA_ROPE = 32
MLA_V = 64
ROPE_BASE = 10000.0
MEM_HEADS = 4
MEM_HEAD_DIM = 128
N_EXPERTS = 16
EC_FACTOR = 2
NA_W = NA_HEADS * NA_HEAD_DIM
MEM_W = MEM_HEADS * MEM_HEAD_DIM
LANES = 128
NEG = -1e30
VMEM_LIMIT = 56 * 1024 * 1024

F32 = jnp.float32
BF16 = jnp.bfloat16


def _dot(a, b):
    return jnp.dot(a, b, preferred_element_type=F32)


def _dot_nt(a, b):
    return lax.dot_general(a, b, (((1,), (1,)), ((), ())), preferred_element_type=F32)


def _rms(xf, g):
    return xf * lax.rsqrt(jnp.mean(xf * xf, axis=-1, keepdims=True) + EPS) * g


def _sigmoid(v):
    return 1.0 / (1.0 + jnp.exp(-v))


def _full(shape):
    return pl.BlockSpec(shape, lambda *_: (0,) * len(shape))


def _params(n_axes):
    return pltpu.CompilerParams(dimension_semantics=("arbitrary",) * n_axes,
                                vmem_limit_bytes=VMEM_LIMIT)


def _inproj_kernel(x_ref, gmix_ref, wna_ref, wcq_ref, wckv_ref, wkr_ref, wqm_ref, qn_ref, kvn_ref,
                   wuq_ref, wukv_ref, cos_ref, sin_ref,
                   na_ref, qmla_ref, kmla_ref, vmla_ref, qmem_ref, *, mla_scale, mem_scale):
    n = _rms(x_ref[...], gmix_ref[...]).astype(BF16)
    z_na = _dot(n, wna_ref[...])
    na_ref[:, :NA_W] = (z_na[:, :NA_W] * (NA_HEAD_DIM ** -0.5)).astype(BF16)
    na_ref[:, NA_W:] = z_na[:, NA_W:].astype(BF16)
    qmem_ref[...] = (_dot(n, wqm_ref[...]) * mem_scale).astype(BF16)

    cos = cos_ref[...]
    sin = sin_ref[...]
    zkr = _dot(n, wkr_ref[...])
    kpe = zkr[:, :LANES] * cos + zkr[:, LANES:] * sin

    cqn = _rms(_dot(n, wcq_ref[...]), qn_ref[...]).astype(BF16)
    zq = _dot(cqn, wuq_ref[...])
    hw = MLA_HEADS * LANES
    for h in range(MLA_HEADS):
        a = zq[:, h * LANES:(h + 1) * LANES]
        b = zq[:, hw + h * LANES:hw + (h + 1) * LANES]
        qmla_ref[:, h * LANES:(h + 1) * LANES] = ((a * cos + b * sin) * mla_scale).astype(BF16)

    ckvn = _rms(_dot(n, wckv_ref[...]), kvn_ref[...]).astype(BF16)
    zkv = _dot(ckvn, wukv_ref[...])
    for h in range(MLA_HEADS):
        kmla_ref[:, h * LANES:(h + 1) * LANES] = (zkv[:, h * LANES:(h + 1) * LANES] + kpe).astype(BF16)
    is_v = lax.broadcasted_iota(jnp.int32, (zkv.shape[0], LANES), 1) < MLA_V
    for h in range(MLA_HEADS):
        blk = zkv[:, hw + h * LANES:hw + (h + 1) * LANES]
        vmla_ref[:, h * LANES:(h + 1) * LANES] = jnp.where(is_v, blk, 1.0).astype(BF16)


def _inproj(x2, gmix, wna, wcq, wckv, wkr, wqm, qn, kvn, wuq, wukv, cos, sin, *, seq, tm):
    t, d = x2.shape
    nblk = seq // tm
    row = lambda w: pl.BlockSpec((tm, w), lambda i: (i, 0))
    rope = pl.BlockSpec((tm, LANES), lambda i: (i % nblk, 0))
    hw = MLA_HEADS * LANES
    kern = functools.partial(_inproj_kernel, mla_scale=(MLA_NOPE + MLA_ROPE) ** -0.5,
                             mem_scale=MEM_HEAD_DIM ** -0.5)
    return pl.pallas_call(
        kern,
        grid=(t // tm,),
        in_specs=[row(d), _full(gmix.shape), _full(wna.shape), _full(wcq.shape), _full(wckv.shape),
                  _full(wkr.shape), _full(wqm.shape), _full(qn.shape), _full(kvn.shape),
                  _full(wuq.shape), _full(wukv.shape), rope, rope],
        out_specs=[row(3 * NA_W), row(hw), row(hw), row(hw), row(MEM_W)],
        out_shape=[jax.ShapeDtypeStruct((t, 3 * NA_W), BF16),
                   jax.ShapeDtypeStruct((t, hw), BF16),
                   jax.ShapeDtypeStruct((t, hw), BF16),
                   jax.ShapeDtypeStruct((t, hw), BF16),
                   jax.ShapeDtypeStruct((t, MEM_W), BF16)],
        compiler_params=_params(1),
        name="inproj",
    )(x2, gmix, wna, wcq, wckv, wkr, wqm, qn, kvn, wuq, wukv, cos, sin)


NA_QROWS = NA_WIN_R // 2
NA_BAND = 12


def _na_kernel(q_ref, k_ref, v_ref, bias_ref, y_ref, *, rows):
    i = pl.program_id(1)
    us = jnp.clip(i * NA_QROWS - NA_WIN_R // 2, 0, rows - NA_BAND)
    start = pl.multiple_of(us * GRID_W, GRID_W)
    band = NA_BAND * GRID_W
    first = lax.broadcasted_iota(jnp.int32, (NA_QROWS * GRID_W, LANES), 1) < NA_HEAD_DIM
    kfirst = lax.broadcasted_iota(jnp.int32, (band, LANES), 1) < NA_HEAD_DIM
    for hp in range(NA_HEADS // 2):
        cols = slice(hp * LANES, (hp + 1) * LANES)
        qp = q_ref[:, cols]
        kp = k_ref[pl.ds(start, band), cols]
        vp = v_ref[pl.ds(start, band), cols]
        outs = []
        for s in range(2):
            mine = first if s == 0 else ~first
            qm = jnp.where(mine, qp, jnp.zeros_like(qp))
            lg = _dot_nt(qm, kp) + bias_ref[0, 2 * hp + s]
            p = jnp.exp((lg - jnp.max(lg, axis=1, keepdims=True)).astype(BF16))
            oa = _dot(p, jnp.where(kfirst if s == 0 else ~kfirst, vp, jnp.ones_like(vp)))
            other = (1 - s) * NA_HEAD_DIM
            outs.append(oa * (1.0 / oa[:, other:other + 1]))
        y_ref[:, cols] = jnp.where(first, outs[0], outs[1]).astype(BF16)


def _na_bias_table(rpb):
    q, bn, wr, wc = NA_QROWS, NA_BAND, NA_WIN_R, NA_WIN_C
    rho = np.arange(q)
    own = np.stack([0 * rho, rho, 0 * rho + (bn - wr)])
    dd = np.stack([rho, 0 * rho + wr // 2, rho + wr // 2])
    ip = np.arange(bn)[None, None, :] - own[:, :, None]
    rvalid = (ip >= 0) & (ip < wr)
    dr = ip - dd[:, :, None] + (wr - 1)
    rsel = (dr[..., None] == np.arange(2 * wr - 1)) & rvalid[..., None]
    c = np.arange(GRID_W)[:, None]
    j = np.arange(GRID_W)[None, :]
    cstart = np.clip(c - wc // 2, 0, GRID_W - wc)
    cvalid = (j >= cstart) & (j < cstart + wc)
    csel = ((j - c + wc - 1)[None] == np.arange(2 * wc - 1)[:, None, None]) & cvalid[None]
    tab = jnp.einsum("hrk,vpir,kcj->vhpcij", rpb.astype(F32), jnp.asarray(rsel, F32), jnp.asarray(csel, F32),
                     precision=lax.Precision.HIGHEST)
    valid = rvalid[:, None, :, None, :, None] & cvalid[None, None, None, :, None, :]
    tab = jnp.where(jnp.asarray(valid), tab, NEG)
    return tab.reshape(3, NA_HEADS, q * GRID_W, bn * GRID_W)


def _na_attention(na, bias, *, batch, seq):
    rows = seq // GRID_W
    nblk = rows // NA_QROWS
    assert rows % NA_QROWS == 0 and rows >= NA_BAND and NA_BAND >= NA_QROWS + NA_WIN_R - 1
    nq = NA_QROWS * GRID_W

    def bias_map(b, i):
        return (jnp.where(i == 0, 0, jnp.where(i == nblk - 1, 2, 1)), 0, 0, 0)

    return pl.pallas_call(
        functools.partial(_na_kernel, rows=rows),
        grid=(batch, nblk),
        in_specs=[pl.BlockSpec((nq, NA_W), lambda b, i: (b * nblk + i, 0)),
                  pl.BlockSpec((seq, NA_W), lambda b, i: (b, 1)),
                  pl.BlockSpec((seq, NA_W), lambda b, i: (b, 2)),
                  pl.BlockSpec((1, NA_HEADS, nq, NA_BAND * GRID_W), bias_map)],
        out_specs=pl.BlockSpec((nq, NA_W), lambda b, i: (b * nblk + i, 0)),
        out_shape=jax.ShapeDtypeStruct((batch * seq, NA_W), BF16),
        compiler_params=_params(2),
        name="na_attn",
    )(na, na, na, bias)


def _mla_kernel(q_ref, k_ref, v_ref, o_ref):
    tq = q_ref.shape[0]
    first = lax.broadcasted_iota(jnp.int32, (tq, LANES), 1) < MLA_V
    for pr in range(MLA_HEADS_PER_STEP // 2):
        outs = []
        for s in range(2):
            cols = slice((2 * pr + s) * LANES, (2 * pr + s + 1) * LANES)
            lg = _dot_nt(q_ref[:, cols], k_ref[:, cols])
            p = jnp.exp((lg - jnp.max(lg, axis=1, keepdims=True)).astype(BF16))
            oa = _dot(p, v_ref[:, cols])
            outs.append(oa * (1.0 / oa[:, MLA_V:MLA_V + 1]))
        o_ref[:, pr * LANES:(pr + 1) * LANES] = jnp.where(
            first, outs[0], pltpu.roll(outs[1], MLA_V, 1)).astype(BF16)


MLA_HEADS_PER_STEP = 4


def _mla_attention(q, k, v, *, batch, seq, tq):
    nq = seq // tq
    hps = MLA_HEADS_PER_STEP
    hp = MLA_HEADS // hps
    return pl.pallas_call(
        _mla_kernel,
        grid=(batch, hp, nq),
        in_specs=[pl.BlockSpec((tq, hps * LANES), lambda b, h, i: (b * nq + i, h)),
                  pl.BlockSpec((seq, hps * LANES), lambda b, h, i: (b, h)),
                  pl.BlockSpec((seq, hps * LANES), lambda b, h, i: (b, h))],
        out_specs=pl.BlockSpec((tq, hps * MLA_V), lambda b, h, i: (b * nq + i, h)),
        out_shape=jax.ShapeDtypeStruct((batch * seq, MLA_HEADS * MLA_V), BF16),
        compiler_params=_params(3),
        name="mla_attn",
    )(q, k, v)


def _memkv_kernel(mem_ref, g_ref, w_ref, kv_ref):
    kv_ref[...] = _dot(_rms(mem_ref[...], g_ref[...]).astype(BF16), w_ref[...]).astype(BF16)


def _memkv(mem2, g, w, *, batch, n_mem):
    d = mem2.shape[1]
    return pl.pallas_call(
        _memkv_kernel,
        grid=(batch,),
        in_specs=[pl.BlockSpec((n_mem, d), lambda b: (b, 0)), _full(g.shape), _full(w.shape)],
        out_specs=pl.BlockSpec((n_mem, 2 * MEM_W), lambda b: (b, 0)),
        out_shape=jax.ShapeDtypeStruct((batch * n_mem, 2 * MEM_W), BF16),
        compiler_params=_params(1),
        name="memkv",
    )(mem2, g, w)


def _merge_kernel(x_ref, yna_ref, ymla_ref, qmem_ref, kvm_ref, gmix_ref, wg_ref, wona_ref, womla_ref,
                  womem_ref, wout_ref, gffn_ref, wrh_ref, wrl_ref, br_ref, h1_ref, xn_ref, lg_ref):
    x = x_ref[...]
    d = x.shape[1]
    n = _rms(x, gmix_ref[...]).astype(BF16)

    ymem = []
    for h in range(MEM_HEADS):
        cols = slice(h * LANES, (h + 1) * LANES)
        lg = _dot_nt(qmem_ref[:, cols], kvm_ref[:, cols])
        p = jnp.exp(lg - jnp.max(lg, axis=1, keepdims=True))
        l = jnp.sum(p, axis=1, keepdims=True)
        vh = kvm_ref[:, MEM_W + h * LANES:MEM_W + (h + 1) * LANES]
        ymem.append((_dot(p.astype(BF16), vh) * (1.0 / l)).astype(BF16))
    ymem = jnp.concatenate(ymem, axis=1)

    merged = None
    for j, (y, w_ref) in enumerate(((yna_ref[...], wona_ref), (ymla_ref[...], womla_ref), (ymem, womem_ref))):
        gate = _sigmoid(_dot(n, wg_ref[:, j * d:(j + 1) * d]))
        term = gate * _dot(y, w_ref[...])
        merged = term if merged is None else merged + term
    h1 = x + _dot(merged.astype(BF16), wout_ref[...])
    h1_ref[...] = h1

    xn = _rms(h1, gffn_ref[...])
    hi = lax.bitcast_convert_type(lax.bitcast_convert_type(xn, jnp.int32) & jnp.int32(-65536), F32)
    xhi = hi.astype(BF16)
    xlo = (xn - hi).astype(BF16)
    wsum_lo = _dot(xhi, wrl_ref[...]) + _dot(xlo, wrh_ref[...]) + _dot(xlo, wrl_ref[...])
    lg_ref[...] = _dot(xhi, wrh_ref[...]) + wsum_lo + br_ref[...]
    xn_ref[...] = xn


def _merge(x2, yna, ymla, qmem, kvm, gmix, wg, wona, womla, womem, wout, gffn, wrh, wrl, br, *, seq, n_mem, tm):
    t, d = x2.shape
    nblk = seq // tm
    row = lambda w: pl.BlockSpec((tm, w), lambda i: (i, 0))
    return pl.pallas_call(
        _merge_kernel,
        grid=(t // tm,),
        in_specs=[row(d), row(NA_W), row(MLA_HEADS * MLA_V), row(MEM_W),
                  pl.BlockSpec((n_mem, 2 * MEM_W), lambda i: (i // nblk, 0)),
                  _full(gmix.shape), _full(wg.shape), _full(wona.shape), _full(womla.shape),
                  _full(womem.shape), _full(wout.shape), _full(gffn.shape), _full(wrh.shape),
                  _full(wrl.shape), _full(br.shape)],
        out_specs=[row(d), row(d), row(LANES)],
        out_shape=[jax.ShapeDtypeStruct((t, d), F32),
                   jax.ShapeDtypeStruct((t, d), F32),
                   jax.ShapeDtypeStruct((t, LANES), F32)],
        compiler_params=_params(1),
        name="merge",
    )(x2, yna, ymla, qmem, kvm, gmix, wg, wona, womla, womem, wout, gffn, wrh, wrl, br)


def _route_kernel(lg_ref, idx_ref, aff_ref, bnd_ref, cs_ref, *, cap, nsplit, tb):
    s = lg_ref.shape[0]
    lg = lg_ref[...]
    ex = jnp.exp(lg - jnp.max(lg, axis=1, keepdims=True))
    aff = ex / jnp.sum(ex, axis=1, keepdims=True)
    aff_ref[...] = aff

    aff_t = aff.T[:N_EXPERTS, :]

    def search(i, lo):
        cand = lo | lax.shift_left(jnp.int32(1), 30 - i)
        cnt = jnp.sum((aff_t >= lax.bitcast_convert_type(cand, F32)).astype(jnp.int32), axis=1, keepdims=True)
        return jnp.where(cnt >= cap, cand, lo)

    thr_col = lax.bitcast_convert_type(
        lax.fori_loop(0, 31, search, jnp.zeros((N_EXPERTS, 1), jnp.int32)), F32)
    thr_sq = jnp.concatenate([jnp.broadcast_to(thr_col, (N_EXPERTS, LANES)),
                              jnp.zeros((LANES - N_EXPERTS, LANES), F32)], axis=0)
    thr = thr_sq.T[0:1, :]
    gt = aff > thr
    eq = aff == thr
    need = cap - jnp.sum(gt.astype(jnp.int32), axis=0, keepdims=True)

    tri = (lax.broadcasted_iota(jnp.int32, (tb, tb), 0)
           >= lax.broadcasted_iota(jnp.int32, (tb, tb), 1)).astype(BF16)

    def cumsum_blocks(mask_of_block):
        carry = jnp.zeros((1, LANES), F32)
        for blk in range(s // tb):
            rows = slice(blk * tb, (blk + 1) * tb)
            c = _dot(tri, mask_of_block(rows).astype(BF16)) + carry
            cs_ref[rows, :] = c
            carry = c[tb - 1:tb, :]

    cumsum_blocks(lambda rows: eq[rows, :])
    eq_rank = cs_ref[...] - eq.astype(F32)
    sel = gt | (eq & (eq_rank < need.astype(F32)))
    cumsum_blocks(lambda rows: sel[rows, :])

    sq = s // nsplit
    sub = lax.broadcasted_iota(jnp.int32, (8, LANES), 0)
    bnd = jnp.zeros((8, LANES), jnp.int32)
    for k in range(1, nsplit + 1):
        bnd = jnp.where(sub == k, cs_ref[k * sq - 1:k * sq, :].astype(jnp.int32), bnd)
    bnd_ref[0] = bnd

    jrow = lax.broadcasted_iota(jnp.int32, (1, cap), 1).astype(F32)
    ones = jnp.ones((8, tb), BF16)
    for e in range(N_EXPERTS):
        cnt = jnp.zeros((8, cap), F32)
        for blk in range(s // tb):
            col = cs_ref[blk * tb:(blk + 1) * tb, e:e + 1]
            cnt = cnt + _dot(ones, jnp.where(col <= jrow, 1.0, 0.0).astype(BF16))
        idx_ref[0, e:e + 1, :] = cnt[0:1, :].astype(jnp.int32)


def _route(lg, *, batch, seq, cap, nsplit):
    tb = min(512, seq)
    return pl.pallas_call(
        functools.partial(_route_kernel, cap=cap, nsplit=nsplit, tb=tb),
        grid=(batch,),
        in_specs=[pl.BlockSpec((seq, LANES), lambda b: (b, 0))],
        out_specs=[pl.BlockSpec((1, N_EXPERTS, cap), lambda b: (b, 0, 0)),
                   pl.BlockSpec((seq, LANES), lambda b: (b, 0)),
                   pl.BlockSpec((1, 8, LANES), lambda b: (b, 0, 0))],
        out_shape=[jax.ShapeDtypeStruct((batch, N_EXPERTS, cap), jnp.int32),
                   jax.ShapeDtypeStruct((batch * seq, LANES), F32),
                   jax.ShapeDtypeStruct((batch, 8, LANES), jnp.int32)],
        scratch_shapes=[pltpu.VMEM((seq, LANES), F32)],
        compiler_params=_params(1),
        name="route",
    )(lg)


EXPERT_FF_CHUNKS = 4


def _expert_kernel(idx_ref, xn_ref, aff_ref, wg_ref, wu_ref, wd_ref, yg_ref, xa_ref, ga_ref, xb_ref, gb_ref,
                   *, cap):
    e = pl.program_id(1)
    last = pl.num_programs(1) - 1
    ff = wg_ref.shape[2]
    fc = ff // EXPERT_FF_CHUNKS
    rc = cap // EXPERT_FF_CHUNKS

    def gather_row(ex, i, xs_ref, gs_ref):
        t = idx_ref[ex, 0, i]
        xs_ref[pl.ds(i, 1), :] = xn_ref[pl.ds(t, 1), :]
        gs_ref[pl.ds(i, 1), :] = aff_ref[pl.ds(t, 1), :]

    @pl.when(e == 0)
    def _():
        def body(i, c):
            gather_row(0, i, xa_ref, ga_ref)
            return c
        lax.fori_loop(0, cap, body, 0, unroll=8)

    def compute(cur_x, cur_g, nxt_x, nxt_g):
        nxt = jnp.minimum(e + 1, last)
        xs = cur_x[...].astype(BF16)
        y = None
        for c in range(EXPERT_FF_CHUNKS):
            fcols = slice(c * fc, (c + 1) * fc)
            hg = _dot(xs, wg_ref[0, :, fcols])
            hu = _dot(xs, wu_ref[0, :, fcols])
            act = (hg * _sigmoid(hg) * hu).astype(BF16)
            part = _dot(act, wd_ref[0, fcols, :])
            y = part if y is None else y + part
            for i in range(c * rc, (c + 1) * rc):
                gather_row(nxt, i, nxt_x, nxt_g)
        lane = lax.broadcasted_iota(jnp.int32, cur_g.shape, 1)
        g = jnp.sum(jnp.where(lane == e, cur_g[...], 0.0), axis=1, keepdims=True)
        yg_ref[0] = y * g

    @pl.when(e % 2 == 0)
    def _():
        compute(xa_ref, ga_ref, xb_ref, gb_ref)

    @pl.when(e % 2 == 1)
    def _():
        compute(xb_ref, gb_ref, xa_ref, ga_ref)


def _experts(idx3, xn, aff, wg, wu, wd, *, batch, seq, cap):
    d = wg.shape[1]
    ff = wg.shape[2]
    ne = N_EXPERTS
    once = pl.Buffered(1)
    return pl.pallas_call(
        functools.partial(_expert_kernel, cap=cap),
        grid=(batch, ne),
        in_specs=[pl.BlockSpec((ne, 1, cap), lambda b, e: (b, 0, 0), memory_space=pltpu.SMEM),
                  pl.BlockSpec((seq, d), lambda b, e: (b, 0), pipeline_mode=once),
                  pl.BlockSpec((seq, LANES), lambda b, e: (b, 0), pipeline_mode=once),
                  pl.BlockSpec((1, d, ff), lambda b, e: (e, 0, 0)),
                  pl.BlockSpec((1, d, ff), lambda b, e: (e, 0, 0)),
                  pl.BlockSpec((1, ff, d), lambda b, e: (e, 0, 0))],
        out_specs=pl.BlockSpec((1, cap, d), lambda b, e: (b * ne + e, 0, 0)),
        out_shape=jax.ShapeDtypeStruct((batch * ne, cap, d), F32),
        scratch_shapes=[pltpu.VMEM((cap, d), F32), pltpu.VMEM((cap, LANES), F32),
                        pltpu.VMEM((cap, d), F32), pltpu.VMEM((cap, LANES), F32)],
        compiler_params=_params(2),
        name="experts",
    )(idx3, xn, aff, wg, wu, wd)


def _combine_kernel(idx_ref, bnd_ref, h1_ref, yg_ref, gfin_ref, out_ref, *, sq):
    q = pl.program_id(1)
    e = pl.program_id(2)

    @pl.when(e == 0)
    def _():
        out_ref[...] = h1_ref[...]

    base = q * sq

    def scatter(i, c):
        t = idx_ref[0, 0, i] - base
        out_ref[pl.ds(t, 1), :] += yg_ref[0, pl.ds(i, 1), :]
        return c

    def scatter4(k, c):
        i = lo + 4 * k
        ts = [idx_ref[0, 0, i + u] - base for u in range(4)]
        new = [out_ref[pl.ds(ts[u], 1), :] + yg_ref[0, pl.ds(i + u, 1), :] for u in range(4)]
        for u in range(4):
            out_ref[pl.ds(ts[u], 1), :] = new[u]
        return c

    lo = bnd_ref[0, q, e]
    hi = bnd_ref[0, q + 1, e]
    quads = lax.shift_right_logical(hi - lo, 2)
    lax.fori_loop(0, quads, scatter4, 0)
    lax.fori_loop(lo + 4 * quads, hi, scatter, 0)

    @pl.when(e == pl.num_programs(2) - 1)
    def _():
        out_ref[...] = _rms(out_ref[...], gfin_ref[...])


def _combine(idx3, bnd, h1, yg, gfin, *, batch, seq, cap, nsplit):
    d = h1.shape[1]
    ne = N_EXPERTS
    sq = seq // nsplit
    return pl.pallas_call(
        functools.partial(_combine_kernel, sq=sq),
        grid=(batch, nsplit, ne),
        in_specs=[pl.BlockSpec((1, 1, cap), lambda b, q, e: (b * ne + e, 0, 0), memory_space=pltpu.SMEM),
                  pl.BlockSpec((1, 8, LANES), lambda b, q, e: (b, 0, 0), memory_space=pltpu.SMEM),
                  pl.BlockSpec((sq, d), lambda b, q, e: (b * nsplit + q, 0)),
                  pl.BlockSpec((1, cap, d), lambda b, q, e: (b * ne + e, 0, 0)),
                  _full(gfin.shape)],
        out_specs=pl.BlockSpec((sq, d), lambda b, q, e: (b * nsplit + q, 0)),
        out_shape=jax.ShapeDtypeStruct((batch * seq, d), F32),
        compiler_params=_params(3),
        name="combine",
    )(idx3, bnd, h1, yg, gfin)


def _rope_tables(seq):
    t = jnp.arange(seq)
    half = MLA_ROPE // 2
    inv = ROPE_BASE ** (-jnp.arange(0, half, 2, dtype=F32) / half)
    ang_r = (t // GRID_W).astype(F32)[:, None] * inv[None, :]
    ang_c = (t % GRID_W).astype(F32)[:, None] * inv[None, :]
    cos = jnp.concatenate([jnp.cos(ang_r), jnp.cos(ang_r), jnp.cos(ang_c), jnp.cos(ang_c)], axis=1)
    sin = jnp.concatenate([-jnp.sin(ang_r), jnp.sin(ang_r), -jnp.sin(ang_c), jnp.sin(ang_c)], axis=1)
    pad = LANES - MLA_NOPE - MLA_ROPE
    cos = jnp.concatenate([jnp.ones((seq, MLA_NOPE), F32), cos, jnp.zeros((seq, pad), F32)], axis=1)
    sin = jnp.concatenate([jnp.zeros((seq, MLA_NOPE), F32), sin, jnp.zeros((seq, pad), F32)], axis=1)
    return cos, sin


_q = MLA_ROPE // 4
ROPE_SWAP = np.concatenate([np.arange(_q, 2 * _q), np.arange(0, _q),
                            np.arange(3 * _q, 4 * _q), np.arange(2 * _q, 3 * _q)])


def kernel(x, mem, g_mix, w_in, na_rpb, mla_q_norm, w_mla_uq, mla_kv_norm, w_mla_ukv, g_mem, w_mem_kv,
           w_o_na, w_o_mla, w_o_mem, w_out, g_ffn, w_router, b_router, w_gate, w_up, w_down, g_final):
    batch, seq, d = x.shape
    n_mem = mem.shape[1]
    depth = g_mix.shape[0]
    q_rank = mla_q_norm.shape[1]
    kv_rank = mla_kv_norm.shape[1]
    cap = EC_FACTOR * seq // N_EXPERTS
    nsplit = 2
    tm = 512
    pad = LANES - MLA_NOPE - MLA_ROPE
    cos, sin = _rope_tables(seq)

    assert depth == 1
    h = x.reshape(batch * seq, d)
    for l in range(depth):
        w = w_in[l]
        o = np.cumsum([0, NA_W, NA_W, NA_W, q_rank, kv_rank, MLA_ROPE, MEM_W, 3 * d])
        wna = w[:, o[0]:o[3]].astype(BF16)
        wcq = w[:, o[3]:o[4]].astype(BF16)
        wckv = w[:, o[4]:o[5]].astype(BF16)
        wkr_raw = w[:, o[5]:o[6]]
        wqm = w[:, o[6]:o[7]].astype(BF16)
        wgate_cols = w[:, o[7]:o[8]].astype(BF16)
        z = lambda *s: jnp.zeros(s, F32)
        wkr = jnp.concatenate([z(d, MLA_NOPE), wkr_raw, z(d, pad),
                               z(d, MLA_NOPE), wkr_raw[:, ROPE_SWAP], z(d, pad)], axis=1).astype(BF16)
        uq = w_mla_uq[l].reshape(q_rank, MLA_HEADS, MLA_NOPE + MLA_ROPE)
        uq_a = jnp.concatenate([uq, z(q_rank, MLA_HEADS, pad)], axis=2)
        uq_b = jnp.concatenate([z(q_rank, MLA_HEADS, MLA_NOPE), uq[:, :, MLA_NOPE:][:, :, ROPE_SWAP],
                                z(q_rank, MLA_HEADS, pad)], axis=2)
        wuq = jnp.concatenate([uq_a.reshape(q_rank, -1), uq_b.reshape(q_rank, -1)], axis=1).astype(BF16)
        ukv = w_mla_ukv[l].reshape(kv_rank, MLA_HEADS, MLA_NOPE + MLA_V)
        uk = jnp.concatenate([ukv[:, :, :MLA_NOPE], z(kv_rank, MLA_HEADS, LANES - MLA_NOPE)], axis=2)
        uv = jnp.concatenate([ukv[:, :, MLA_NOPE:], z(kv_rank, MLA_HEADS, LANES - MLA_V)], axis=2)
        wukv = jnp.concatenate([uk.reshape(kv_rank, -1), uv.reshape(kv_rank, -1)], axis=1).astype(BF16)
        wr = jnp.concatenate([w_router[l].astype(F32), z(d, LANES - N_EXPERTS)], axis=1)
        wr_hi = lax.bitcast_convert_type(lax.bitcast_convert_type(wr, jnp.int32) & jnp.int32(-65536), F32)
        wrh = wr_hi.astype(BF16)
        wrl = (wr - wr_hi).astype(BF16)
        br = jnp.concatenate([b_router[l].astype(F32), jnp.full((LANES - N_EXPERTS,), NEG, F32)])[None, :]

        na, qmla, kmla, vmla, qmem = _inproj(
            h, g_mix[l][None, :], wna, wcq, wckv, wkr, wqm, mla_q_norm[l][None, :], mla_kv_norm[l][None, :],
            wuq, wukv, cos, sin, seq=seq, tm=tm)
        y_na = _na_attention(na, _na_bias_table(na_rpb[l]), batch=batch, seq=seq)
        y_mla = _mla_attention(qmla, kmla, vmla, batch=batch, seq=seq, tq=256)
        kvm = _memkv(mem.reshape(batch * n_mem, d), g_mem[l][None, :], w_mem_kv[l].astype(BF16),
                     batch=batch, n_mem=n_mem)
        h1, xn, lg = _merge(h, y_na, y_mla, qmem, kvm, g_mix[l][None, :], wgate_cols,
                             w_o_na[l].astype(BF16), w_o_mla[l].astype(BF16), w_o_mem[l].astype(BF16),
                             w_out[l].astype(BF16), g_ffn[l][None, :], wrh, wrl, br,
                             seq=seq, n_mem=n_mem, tm=256)
        idx, aff, bnd = _route(lg, batch=batch, seq=seq, cap=cap, nsplit=nsplit)
        idx3 = idx.reshape(batch * N_EXPERTS, 1, cap)
        yg = _experts(idx3, xn, aff, w_gate[l].astype(BF16), w_up[l].astype(BF16), w_down[l].astype(BF16),
                      batch=batch, seq=seq, cap=cap)
        h = _combine(idx3, bnd, h1, yg, g_final[None, :], batch=batch, seq=seq, cap=cap, nsplit=nsplit)
    return h.reshape(batch, seq, d)
```

```python
import functools

import jax
import jax.numpy as jnp
import numpy as np
from jax import lax
from jax.experimental import pallas as pl
from jax.experimental.pallas import tpu as pltpu

GRID_W = 64
EPS = 1e-6
NA_HEADS = 8
NA_HEAD_DIM = 64
NA_WIN_R = 8
NA_WIN_C = 16
MLA_HEADS = 8
MLA_NOPE = 64
MLA_ROPE = 32
MLA_V = 64
ROPE_BASE = 10000.0
MEM_HEADS = 4
MEM_HEAD_DIM = 128
N_EXPERTS = 16
EC_FACTOR = 2
NA_W = NA_HEADS * NA_HEAD_DIM
MEM_W = MEM_HEADS * MEM_HEAD_DIM
LANES = 128
NEG = -1e30
VMEM_LIMIT = 56 * 1024 * 1024
EXPERTS_VMEM_LIMIT = 62 * 1024 * 1024

F32 = jnp.float32
BF16 = jnp.bfloat16


def _dot(a, b):
    return jnp.dot(a, b, preferred_element_type=F32)


def _dot_nt(a, b):
    return lax.dot_general(a, b, (((1,), (1,)), ((), ())), preferred_element_type=F32)


def _rms(xf, g):
    return xf * lax.rsqrt(jnp.mean(xf * xf, axis=-1, keepdims=True) + EPS) * g


def _sigmoid(v):
    return 1.0 / (1.0 + jnp.exp(-v))


def _full(shape):
    return pl.BlockSpec(shape, lambda *_: (0,) * len(shape))


def _params(n_axes):
    return pltpu.CompilerParams(dimension_semantics=("arbitrary",) * n_axes,
                                vmem_limit_bytes=VMEM_LIMIT)


def _inproj_kernel(x_ref, gmix_ref, wna_ref, wcq_ref, wckv_ref, wkr_ref, wqm_ref, qn_ref, kvn_ref,
                   wuq_ref, wukv_ref, cos_ref, sin_ref,
                   na_ref, qmla_ref, kmla_ref, vmla_ref, qmem_ref, *, mla_scale, mem_scale):
    n = _rms(x_ref[...], gmix_ref[...]).astype(BF16)
    z_na = _dot(n, wna_ref[...])
    na_ref[:, :NA_W] = (z_na[:, :NA_W] * (NA_HEAD_DIM ** -0.5)).astype(BF16)
    na_ref[:, NA_W:] = z_na[:, NA_W:].astype(BF16)
    qmem_ref[...] = (_dot(n, wqm_ref[...]) * mem_scale).astype(BF16)

    cos = cos_ref[...]
    sin = sin_ref[...]
    zkr = _dot(n, wkr_ref[...])
    kpe = zkr[:, :LANES] * cos + zkr[:, LANES:] * sin

    cqn = _rms(_dot(n, wcq_ref[...]), qn_ref[...]).astype(BF16)
    zq = _dot(cqn, wuq_ref[...])
    hw = MLA_HEADS * LANES
    for h in range(MLA_HEADS):
        a = zq[:, h * LANES:(h + 1) * LANES]
        b = zq[:, hw + h * LANES:hw + (h + 1) * LANES]
        qmla_ref[:, h * LANES:(h + 1) * LANES] = ((a * cos + b * sin) * mla_scale).astype(BF16)

    ckvn = _rms(_dot(n, wckv_ref[...]), kvn_ref[...]).astype(BF16)
    zkv = _dot(ckvn, wukv_ref[...])
    for h in range(MLA_HEADS):
        kmla_ref[:, h * LANES:(h + 1) * LANES] = (zkv[:, h * LANES:(h + 1) * LANES] + kpe).astype(BF16)
    is_v = lax.broadcasted_iota(jnp.int32, (zkv.shape[0], LANES), 1) < MLA_V
    for h in range(MLA_HEADS):
        blk = zkv[:, hw + h * LANES:hw + (h + 1) * LANES]
        vmla_ref[:, h * LANES:(h + 1) * LANES] = jnp.where(is_v, blk, 1.0).astype(BF16)


def _inproj(x2, gmix, wna, wcq, wckv, wkr, wqm, qn, kvn, wuq, wukv, cos, sin, *, seq, tm):
    t, d = x2.shape
    nblk = seq // tm
    row = lambda w: pl.BlockSpec((tm, w), lambda i: (i, 0))
    rope = pl.BlockSpec((tm, LANES), lambda i: (i % nblk, 0))
    hw = MLA_HEADS * LANES
    kern = functools.partial(_inproj_kernel, mla_scale=(MLA_NOPE + MLA_ROPE) ** -0.5,
                             mem_scale=MEM_HEAD_DIM ** -0.5)
    return pl.pallas_call(
        kern,
        grid=(t // tm,),
        in_specs=[row(d), _full(gmix.shape), _full(wna.shape), _full(wcq.shape), _full(wckv.shape),
                  _full(wkr.shape), _full(wqm.shape), _full(qn.shape), _full(kvn.shape),
                  _full(wuq.shape), _full(wukv.shape), rope, rope],
        out_specs=[row(3 * NA_W), row(hw), row(hw), row(hw), row(MEM_W)],
        out_shape=[jax.ShapeDtypeStruct((t, 3 * NA_W), BF16),
                   jax.ShapeDtypeStruct((t, hw), BF16),
                   jax.ShapeDtypeStruct((t, hw), BF16),
                   jax.ShapeDtypeStruct((t, hw), BF16),
                   jax.ShapeDtypeStruct((t, MEM_W), BF16)],
        compiler_params=_params(1),
        name="inproj",
    )(x2, gmix, wna, wcq, wckv, wkr, wqm, qn, kvn, wuq, wukv, cos, sin)


NA_QROWS = NA_WIN_R // 2
NA_BAND = 12


def _na_kernel(q_ref, k_ref, v_ref, bias_ref, y_ref, *, rows):
    i = pl.program_id(1)
    us = jnp.clip(i * NA_QROWS - NA_WIN_R // 2, 0, rows - NA_BAND)
    start = pl.multiple_of(us * GRID_W, GRID_W)
    band = NA_BAND * GRID_W
    first = lax.broadcasted_iota(jnp.int32, (NA_QROWS * GRID_W, LANES), 1) < NA_HEAD_DIM
    kfirst = lax.broadcasted_iota(jnp.int32, (band, LANES), 1) < NA_HEAD_DIM
    for hp in range(NA_HEADS // 2):
        cols = slice(hp * LANES, (hp + 1) * LANES)
        qp = q_ref[:, cols]
        kp = k_ref[pl.ds(start, band), cols]
        vp = v_ref[pl.ds(start, band), cols]
        outs = []
        for s in range(2):
            mine = first if s == 0 else ~first
            qm = jnp.where(mine, qp, jnp.zeros_like(qp))
            lg = _dot_nt(qm, kp) + bias_ref[0, 2 * hp + s]
            p = jnp.exp((lg - jnp.max(lg, axis=1, keepdims=True)).astype(BF16))
            oa = _dot(p, jnp.where(kfirst if s == 0 else ~kfirst, vp, jnp.ones_like(vp)))
            other = (1 - s) * NA_HEAD_DIM
            outs.append(oa * (1.0 / oa[:, other:other + 1]))
        y_ref[:, cols] = jnp.where(first, outs[0], outs[1]).astype(BF16)


def _na_bias_table(rpb):
    q, bn, wr, wc = NA_QROWS, NA_BAND, NA_WIN_R, NA_WIN_C
    rho = np.arange(q)
    own = np.stack([0 * rho, rho, 0 * rho + (bn - wr)])
    dd = np.stack([rho, 0 * rho + wr // 2, rho + wr // 2])
    ip = np.arange(bn)[None, None, :] - own[:, :, None]
    rvalid = (ip >= 0) & (ip < wr)
    dr = ip - dd[:, :, None] + (wr - 1)
    rsel = np.where(rvalid, dr, 2 * wr - 1)[..., None] == np.arange(2 * wr)
    c = np.arange(GRID_W)[:, None]
    j = np.arange(GRID_W)[None, :]
    cstart = np.clip(c - wc // 2, 0, GRID_W - wc)
    cvalid = (j >= cstart) & (j < cstart + wc)
    csel = np.where(cvalid, j - c + wc - 1, 2 * wc - 1)[None] == np.arange(2 * wc)[:, None, None]
    ext = jnp.pad(rpb.astype(F32), ((0, 0), (0, 1), (0, 1)), constant_values=NEG)
    tab = jnp.einsum("hrk,vpir,kcj->vhpcij", ext, jnp.asarray(rsel, F32), jnp.asarray(csel, F32),
                     precision=lax.Precision.HIGHEST)
    return tab.reshape(3, NA_HEADS, q * GRID_W, bn * GRID_W)


def _na_attention(na, bias, *, batch, seq):
    rows = seq // GRID_W
    nblk = rows // NA_QROWS
    assert rows % NA_QROWS == 0 and rows >= NA_BAND and NA_BAND >= NA_QROWS + NA_WIN_R - 1
    nq = NA_QROWS * GRID_W

    def bias_map(b, i):
        return (jnp.where(i == 0, 0, jnp.where(i == nblk - 1, 2, 1)), 0, 0, 0)

    return pl.pallas_call(
        functools.partial(_na_kernel, rows=rows),
        grid=(batch, nblk),
        in_specs=[pl.BlockSpec((nq, NA_W), lambda b, i: (b * nblk + i, 0)),
                  pl.BlockSpec((seq, NA_W), lambda b, i: (b, 1)),
                  pl.BlockSpec((seq, NA_W), lambda b, i: (b, 2)),
                  pl.BlockSpec((1, NA_HEADS, nq, NA_BAND * GRID_W), bias_map)],
        out_specs=pl.BlockSpec((nq, NA_W), lambda b, i: (b * nblk + i, 0)),
        out_shape=jax.ShapeDtypeStruct((batch * seq, NA_W), BF16),
        compiler_params=_params(2),
        name="na_attn",
    )(na, na, na, bias)


def _mla_kernel(q_ref, k_ref, v_ref, o_ref):
    tq = q_ref.shape[0]
    first = lax.broadcasted_iota(jnp.int32, (tq, LANES), 1) < MLA_V
    for pr in range(MLA_HEADS_PER_STEP // 2):
        outs = []
        for s in range(2):
            cols = slice((2 * pr + s) * LANES, (2 * pr + s + 1) * LANES)
            lg = _dot_nt(q_ref[:, cols], k_ref[:, cols])
            p = jnp.exp((lg - jnp.max(lg, axis=1, keepdims=True)).astype(BF16))
            oa = _dot(p, v_ref[:, cols])
            outs.append(oa * (1.0 / oa[:, MLA_V:MLA_V + 1]))
        o_ref[:, pr * LANES:(pr + 1) * LANES] = jnp.where(
            first, outs[0], pltpu.roll(outs[1], MLA_V, 1)).astype(BF16)


MLA_HEADS_PER_STEP = 4


def _mla_attention(q, k, v, *, batch, seq, tq):
    nq = seq // tq
    hps = MLA_HEADS_PER_STEP
    hp = MLA_HEADS // hps
    return pl.pallas_call(
        _mla_kernel,
        grid=(batch, hp, nq),
        in_specs=[pl.BlockSpec((tq, hps * LANES), lambda b, h, i: (b * nq + i, h)),
                  pl.BlockSpec((seq, hps * LANES), lambda b, h, i: (b, h)),
                  pl.BlockSpec((seq, hps * LANES), lambda b, h, i: (b, h))],
        out_specs=pl.BlockSpec((tq, hps * MLA_V), lambda b, h, i: (b * nq + i, h)),
        out_shape=jax.ShapeDtypeStruct((batch * seq, MLA_HEADS * MLA_V), BF16),
        compiler_params=_params(3),
        name="mla_attn",
    )(q, k, v)


def _memkv_kernel(mem_ref, g_ref, w_ref, kv_ref):
    kv_ref[...] = _dot(_rms(mem_ref[...], g_ref[...]).astype(BF16), w_ref[...]).astype(BF16)


def _memkv(mem2, g, w, *, batch, n_mem):
    d = mem2.shape[1]
    return pl.pallas_call(
        _memkv_kernel,
        grid=(batch,),
        in_specs=[pl.BlockSpec((n_mem, d), lambda b: (b, 0)), _full(g.shape), _full(w.shape)],
        out_specs=pl.BlockSpec((n_mem, 2 * MEM_W), lambda b: (b, 0)),
        out_shape=jax.ShapeDtypeStruct((batch * n_mem, 2 * MEM_W), BF16),
        compiler_params=_params(1),
        name="memkv",
    )(mem2, g, w)


def _merge_kernel(x_ref, yna_ref, ymla_ref, qmem_ref, kvm_ref, gmix_ref, wg_ref, wona_ref, womla_ref,
                  womem_ref, wout_ref, gffn_ref, wrh_ref, wrl_ref, br_ref, h1_ref, xn_ref, lg_ref):
    x = x_ref[...]
    d = x.shape[1]
    n = _rms(x, gmix_ref[...]).astype(BF16)

    ymem = []
    for h in range(MEM_HEADS):
        cols = slice(h * LANES, (h + 1) * LANES)
        lg = _dot_nt(qmem_ref[:, cols], kvm_ref[:, cols])
        p = jnp.exp(lg - jnp.max(lg, axis=1, keepdims=True))
        l = jnp.sum(p, axis=1, keepdims=True)
        vh = kvm_ref[:, MEM_W + h * LANES:MEM_W + (h + 1) * LANES]
        ymem.append((_dot(p.astype(BF16), vh) * (1.0 / l)).astype(BF16))
    ymem = jnp.concatenate(ymem, axis=1)

    merged = None
    for j, (y, w_ref) in enumerate(((yna_ref[...], wona_ref), (ymla_ref[...], womla_ref), (ymem, womem_ref))):
        gate = _sigmoid(_dot(n, wg_ref[:, j * d:(j + 1) * d]))
        term = gate * _dot(y, w_ref[...])
        merged = term if merged is None else merged + term
    h1 = x + _dot(merged.astype(BF16), wout_ref[...])
    h1_ref[...] = h1

    xn = _rms(h1, gffn_ref[...])
    hi = lax.bitcast_convert_type(lax.bitcast_convert_type(xn, jnp.int32) & jnp.int32(-65536), F32)
    xhi = hi.astype(BF16)
    xlo = (xn - hi).astype(BF16)
    wsum_lo = _dot(xhi, wrl_ref[...]) + _dot(xlo, wrh_ref[...]) + _dot(xlo, wrl_ref[...])
    lg_ref[...] = _dot(xhi, wrh_ref[...]) + wsum_lo + br_ref[...]
    xn_ref[...] = xn


def _merge(x2, yna, ymla, qmem, kvm, gmix, wg, wona, womla, womem, wout, gffn, wrh, wrl, br, *, seq, n_mem, tm):
    t, d = x2.shape
    nblk = seq // tm
    row = lambda w: pl.BlockSpec((tm, w), lambda i: (i, 0))
    return pl.pallas_call(
        _merge_kernel,
        grid=(t // tm,),
        in_specs=[row(d), row(NA_W), row(MLA_HEADS * MLA_V), row(MEM_W),
                  pl.BlockSpec((n_mem, 2 * MEM_W), lambda i: (i // nblk, 0)),
                  _full(gmix.shape), _full(wg.shape), _full(wona.shape), _full(womla.shape),
                  _full(womem.shape), _full(wout.shape), _full(gffn.shape), _full(wrh.shape),
                  _full(wrl.shape), _full(br.shape)],
        out_specs=[row(d), row(d), row(LANES)],
        out_shape=[jax.ShapeDtypeStruct((t, d), F32),
                   jax.ShapeDtypeStruct((t, d), F32),
                   jax.ShapeDtypeStruct((t, LANES), F32)],
        compiler_params=_params(1),
        name="merge",
    )(x2, yna, ymla, qmem, kvm, gmix, wg, wona, womla, womem, wout, gffn, wrh, wrl, br)


def _route_kernel(lg_ref, idx_ref, aff_ref, bnd_ref, cs_ref, *, cap, nsplit, tb):
    s = lg_ref.shape[0]
    lg = lg_ref[...]
    ex = jnp.exp(lg - jnp.max(lg, axis=1, keepdims=True))
    aff = ex / jnp.sum(ex, axis=1, keepdims=True)
    aff_ref[...] = aff

    aff_t = aff.T[:N_EXPERTS, :]

    def search(i, lo):
        cand = lo | lax.shift_left(jnp.int32(1), 30 - i)
        cnt = jnp.sum((aff_t >= lax.bitcast_convert_type(cand, F32)).astype(jnp.int32), axis=1, keepdims=True)
        return jnp.where(cnt >= cap, cand, lo)

    thr_col = lax.bitcast_convert_type(
        lax.fori_loop(0, 31, search, jnp.zeros((N_EXPERTS, 1), jnp.int32)), F32)
    thr_sq = jnp.concatenate([jnp.broadcast_to(thr_col, (N_EXPERTS, LANES)),
                              jnp.zeros((LANES - N_EXPERTS, LANES), F32)], axis=0)
    thr = thr_sq.T[0:1, :]
    gt = aff > thr
    eq = aff == thr
    need = cap - jnp.sum(gt.astype(jnp.int32), axis=0, keepdims=True)

    tri = (lax.broadcasted_iota(jnp.int32, (tb, tb), 0)
           >= lax.broadcasted_iota(jnp.int32, (tb, tb), 1)).astype(BF16)

    def cumsum_blocks(mask_of_block):
        carry = jnp.zeros((1, LANES), F32)
        for blk in range(s // tb):
            rows = slice(blk * tb, (blk + 1) * tb)
            c = _dot(tri, mask_of_block(rows).astype(BF16)) + carry
            cs_ref[rows, :] = c
            carry = c[tb - 1:tb, :]

    cumsum_blocks(lambda rows: eq[rows, :])
    eq_rank = cs_ref[...] - eq.astype(F32)
    sel = gt | (eq & (eq_rank < need.astype(F32)))
    cumsum_blocks(lambda rows: sel[rows, :])

    sq = s // nsplit
    sub = lax.broadcasted_iota(jnp.int32, (8, LANES), 0)
    bnd = jnp.zeros((8, LANES), jnp.int32)
    for k in range(1, nsplit + 1):
        bnd = jnp.where(sub == k, cs_ref[k * sq - 1:k * sq, :].astype(jnp.int32), bnd)
    bnd_ref[0] = bnd

    jrow = lax.broadcasted_iota(jnp.int32, (1, cap), 1).astype(F32)
    ones = jnp.ones((8, tb), BF16)
    for e in range(N_EXPERTS):
        cnt = jnp.zeros((8, cap), F32)
        for blk in range(s // tb):
            col = cs_ref[blk * tb:(blk + 1) * tb, e:e + 1]
            cnt = cnt + _dot(ones, jnp.where(col <= jrow, 1.0, 0.0).astype(BF16))
        idx_ref[0, e:e + 1, :] = cnt[0:1, :].astype(jnp.int32)


def _route(lg, *, batch, seq, cap, nsplit):
    tb = min(512, seq)
    return pl.pallas_call(
        functools.partial(_route_kernel, cap=cap, nsplit=nsplit, tb=tb),
        grid=(batch,),
        in_specs=[pl.BlockSpec((seq, LANES), lambda b: (b, 0))],
        out_specs=[pl.BlockSpec((1, N_EXPERTS, cap), lambda b: (b, 0, 0)),
                   pl.BlockSpec((seq, LANES), lambda b: (b, 0)),
                   pl.BlockSpec((1, 8, LANES), lambda b: (b, 0, 0))],
        out_shape=[jax.ShapeDtypeStruct((batch, N_EXPERTS, cap), jnp.int32),
                   jax.ShapeDtypeStruct((batch * seq, LANES), F32),
                   jax.ShapeDtypeStruct((batch, 8, LANES), jnp.int32)],
        scratch_shapes=[pltpu.VMEM((seq, LANES), F32)],
        compiler_params=_params(1),
        name="route",
    )(lg)


EXPERT_FF_CHUNKS = 4


def _expert_kernel(idx_ref, xn_ref, aff_ref, wg_ref, wu_ref, wd_ref, yg_ref, xa_ref, ga_ref, xb_ref, gb_ref,
                   *, cap):
    e = pl.program_id(1)
    last = pl.num_programs(1) - 1
    ff = wg_ref.shape[2]
    fc = ff // EXPERT_FF_CHUNKS
    rc = cap // EXPERT_FF_CHUNKS

    def gather_row(ex, i, xs_ref, gs_ref):
        t = idx_ref[ex, 0, i]
        xs_ref[pl.ds(i, 1), :] = xn_ref[pl.ds(t, 1), :]
        gs_ref[pl.ds(i, 1), :] = aff_ref[pl.ds(t, 1), :]

    @pl.when(e == 0)
    def _():
        def body(i, c):
            gather_row(0, i, xa_ref, ga_ref)
            return c
        lax.fori_loop(0, cap, body, 0, unroll=8)

    def compute(cur_x, cur_g, nxt_x, nxt_g):
        nxt = jnp.minimum(e + 1, last)
        xs = cur_x[...].astype(BF16)
        y = None
        for c in range(EXPERT_FF_CHUNKS):
            fcols = slice(c * fc, (c + 1) * fc)
            hg = _dot(xs, wg_ref[0, :, fcols].astype(BF16))
            hu = _dot(xs, wu_ref[0, :, fcols].astype(BF16))
            act = (hg * _sigmoid(hg) * hu).astype(BF16)
            part = _dot(act, wd_ref[0, fcols, :].astype(BF16))
            y = part if y is None else y + part
            for i in range(c * rc, (c + 1) * rc):
                gather_row(nxt, i, nxt_x, nxt_g)
        lane = lax.broadcasted_iota(jnp.int32, cur_g.shape, 1)
        g = jnp.sum(jnp.where(lane == e, cur_g[...], 0.0), axis=1, keepdims=True)
        yg_ref[0] = y * g

    @pl.when(e % 2 == 0)
    def _():
        compute(xa_ref, ga_ref, xb_ref, gb_ref)

    @pl.when(e % 2 == 1)
    def _():
        compute(xb_ref, gb_ref, xa_ref, ga_ref)


def _experts(idx3, xn, aff, wg, wu, wd, *, batch, seq, cap):
    d = wg.shape[1]
    ff = wg.shape[2]
    ne = N_EXPERTS
    once = pl.Buffered(1)
    return pl.pallas_call(
        functools.partial(_expert_kernel, cap=cap),
        grid=(batch, ne),
        in_specs=[pl.BlockSpec((ne, 1, cap), lambda b, e: (b, 0, 0), memory_space=pltpu.SMEM),
                  pl.BlockSpec((seq, d), lambda b, e: (b, 0), pipeline_mode=once),
                  pl.BlockSpec((seq, LANES), lambda b, e: (b, 0), pipeline_mode=once),
                  pl.BlockSpec((1, d, ff), lambda b, e: (e, 0, 0)),
                  pl.BlockSpec((1, d, ff), lambda b, e: (e, 0, 0)),
                  pl.BlockSpec((1, ff, d), lambda b, e: (e, 0, 0))],
        out_specs=pl.BlockSpec((1, cap, d), lambda b, e: (b * ne + e, 0, 0)),
        out_shape=jax.ShapeDtypeStruct((batch * ne, cap, d), F32),
        scratch_shapes=[pltpu.VMEM((cap, d), F32), pltpu.VMEM((cap, LANES), F32),
                        pltpu.VMEM((cap, d), F32), pltpu.VMEM((cap, LANES), F32)],
        compiler_params=pltpu.CompilerParams(dimension_semantics=("arbitrary", "arbitrary"),
                                             vmem_limit_bytes=EXPERTS_VMEM_LIMIT),
        name="experts",
    )(idx3, xn, aff, wg, wu, wd)


def _combine_kernel(idx_ref, bnd_ref, h1_ref, yg_ref, gfin_ref, out_ref, *, sq):
    q = pl.program_id(1)
    e = pl.program_id(2)

    @pl.when(e == 0)
    def _():
        out_ref[...] = h1_ref[...]

    base = q * sq

    def scatter(i, c):
        t = idx_ref[0, 0, i] - base
        out_ref[pl.ds(t, 1), :] += yg_ref[0, pl.ds(i, 1), :]
        return c

    def scatter4(k, c):
        i = lo + 4 * k
        ts = [idx_ref[0, 0, i + u] - base for u in range(4)]
        new = [out_ref[pl.ds(ts[u], 1), :] + yg_ref[0, pl.ds(i + u, 1), :] for u in range(4)]
        for u in range(4):
            out_ref[pl.ds(ts[u], 1), :] = new[u]
        return c

    lo = bnd_ref[0, q, e]
    hi = bnd_ref[0, q + 1, e]
    quads = lax.shift_right_logical(hi - lo, 2)
    lax.fori_loop(0, quads, scatter4, 0)
    lax.fori_loop(lo + 4 * quads, hi, scatter, 0)

    @pl.when(e == pl.num_programs(2) - 1)
    def _():
        out_ref[...] = _rms(out_ref[...], gfin_ref[...])


def _combine(idx3, bnd, h1, yg, gfin, *, batch, seq, cap, nsplit):
    d = h1.shape[1]
    ne = N_EXPERTS
    sq = seq // nsplit
    return pl.pallas_call(
        functools.partial(_combine_kernel, sq=sq),
        grid=(batch, nsplit, ne),
        in_specs=[pl.BlockSpec((1, 1, cap), lambda b, q, e: (b * ne + e, 0, 0), memory_space=pltpu.SMEM),
                  pl.BlockSpec((1, 8, LANES), lambda b, q, e: (b, 0, 0), memory_space=pltpu.SMEM),
                  pl.BlockSpec((sq, d), lambda b, q, e: (b * nsplit + q, 0)),
                  pl.BlockSpec((1, cap, d), lambda b, q, e: (b * ne + e, 0, 0)),
                  _full(gfin.shape)],
        out_specs=pl.BlockSpec((sq, d), lambda b, q, e: (b * nsplit + q, 0)),
        out_shape=jax.ShapeDtypeStruct((batch * seq, d), F32),
        compiler_params=_params(3),
        name="combine",
    )(idx3, bnd, h1, yg, gfin)


def _rope_tables(seq):
    t = jnp.arange(seq)
    half = MLA_ROPE // 2
    inv = ROPE_BASE ** (-jnp.arange(0, half, 2, dtype=F32) / half)
    ang_r = (t // GRID_W).astype(F32)[:, None] * inv[None, :]
    ang_c = (t % GRID_W).astype(F32)[:, None] * inv[None, :]
    cos = jnp.concatenate([jnp.cos(ang_r), jnp.cos(ang_r), jnp.cos(ang_c), jnp.cos(ang_c)], axis=1)
    sin = jnp.concatenate([-jnp.sin(ang_r), jnp.sin(ang_r), -jnp.sin(ang_c), jnp.sin(ang_c)], axis=1)
    pad = LANES - MLA_NOPE - MLA_ROPE
    cos = jnp.concatenate([jnp.ones((seq, MLA_NOPE), F32), cos, jnp.zeros((seq, pad), F32)], axis=1)
    sin = jnp.concatenate([jnp.zeros((seq, MLA_NOPE), F32), sin, jnp.zeros((seq, pad), F32)], axis=1)
    return cos, sin


_q = MLA_ROPE // 4
ROPE_SWAP = np.concatenate([np.arange(_q, 2 * _q), np.arange(0, _q),
                            np.arange(3 * _q, 4 * _q), np.arange(2 * _q, 3 * _q)])


def kernel(x, mem, g_mix, w_in, na_rpb, mla_q_norm, w_mla_uq, mla_kv_norm, w_mla_ukv, g_mem, w_mem_kv,
           w_o_na, w_o_mla, w_o_mem, w_out, g_ffn, w_router, b_router, w_gate, w_up, w_down, g_final):
    batch, seq, d = x.shape
    n_mem = mem.shape[1]
    depth = g_mix.shape[0]
    q_rank = mla_q_norm.shape[1]
    kv_rank = mla_kv_norm.shape[1]
    cap = EC_FACTOR * seq // N_EXPERTS
    nsplit = 2
    tm = 512
    pad = LANES - MLA_NOPE - MLA_ROPE
    cos, sin = _rope_tables(seq)

    assert depth == 1
    h = x.reshape(batch * seq, d)
    for l in range(depth):
        w = w_in[l]
        o = np.cumsum([0, NA_W, NA_W, NA_W, q_rank, kv_rank, MLA_ROPE, MEM_W, 3 * d])
        wna = w[:, o[0]:o[3]].astype(BF16)
        wcq = w[:, o[3]:o[4]].astype(BF16)
        wckv = w[:, o[4]:o[5]].astype(BF16)
        wkr_raw = w[:, o[5]:o[6]]
        wqm = w[:, o[6]:o[7]].astype(BF16)
        wgate_cols = w[:, o[7]:o[8]].astype(BF16)
        z = lambda *s: jnp.zeros(s, F32)
        wkr = jnp.concatenate([z(d, MLA_NOPE), wkr_raw, z(d, pad),
                               z(d, MLA_NOPE), wkr_raw[:, ROPE_SWAP], z(d, pad)], axis=1).astype(BF16)
        uq = w_mla_uq[l].reshape(q_rank, MLA_HEADS, MLA_NOPE + MLA_ROPE)
        uq_a = jnp.concatenate([uq, z(q_rank, MLA_HEADS, pad)], axis=2)
        uq_b = jnp.concatenate([z(q_rank, MLA_HEADS, MLA_NOPE), uq[:, :, MLA_NOPE:][:, :, ROPE_SWAP],
                                z(q_rank, MLA_HEADS, pad)], axis=2)
        wuq = jnp.concatenate([uq_a.reshape(q_rank, -1), uq_b.reshape(q_rank, -1)], axis=1).astype(BF16)
        ukv = w_mla_ukv[l].reshape(kv_rank, MLA_HEADS, MLA_NOPE + MLA_V)
        uk = jnp.concatenate([ukv[:, :, :MLA_NOPE], z(kv_rank, MLA_HEADS, LANES - MLA_NOPE)], axis=2)
        uv = jnp.concatenate([ukv[:, :, MLA_NOPE:], z(kv_rank, MLA_HEADS, LANES - MLA_V)], axis=2)
        wukv = jnp.concatenate([uk.reshape(kv_rank, -1), uv.reshape(kv_rank, -1)], axis=1).astype(BF16)
        wr = jnp.concatenate([w_router[l].astype(F32), z(d, LANES - N_EXPERTS)], axis=1)
        wr_hi = lax.bitcast_convert_type(lax.bitcast_convert_type(wr, jnp.int32) & jnp.int32(-65536), F32)
        wrh = wr_hi.astype(BF16)
        wrl = (wr - wr_hi).astype(BF16)
        br = jnp.concatenate([b_router[l].astype(F32), jnp.full((LANES - N_EXPERTS,), NEG, F32)])[None, :]

        na, qmla, kmla, vmla, qmem = _inproj(
            h, g_mix[l][None, :], wna, wcq, wckv, wkr, wqm, mla_q_norm[l][None, :], mla_kv_norm[l][None, :],
            wuq, wukv, cos, sin, seq=seq, tm=tm)
        y_na = _na_attention(na, _na_bias_table(na_rpb[l]), batch=batch, seq=seq)
        y_mla = _mla_attention(qmla, kmla, vmla, batch=batch, seq=seq, tq=256)
        kvm = _memkv(mem.reshape(batch * n_mem, d), g_mem[l][None, :], w_mem_kv[l].astype(BF16),
                     batch=batch, n_mem=n_mem)
        h1, xn, lg = _merge(h, y_na, y_mla, qmem, kvm, g_mix[l][None, :], wgate_cols,
                             w_o_na[l].astype(BF16), w_o_mla[l].astype(BF16), w_o_mem[l].astype(BF16),
                             w_out[l].astype(BF16), g_ffn[l][None, :], wrh, wrl, br,
                             seq=seq, n_mem=n_mem, tm=256)
        idx, aff, bnd = _route(lg, batch=batch, seq=seq, cap=cap, nsplit=nsplit)
        idx3 = idx.reshape(batch * N_EXPERTS, 1, cap)
        yg = _experts(idx3, xn, aff, w_gate[l], w_up[l], w_down[l], batch=batch, seq=seq, cap=cap)
        h = _combine(idx3, bnd, h1, yg, g_final[None, :], batch=batch, seq=seq, cap=cap, nsplit=nsplit)
    return h.reshape(batch, seq, d)
```

```python
import functools

import jax
import jax.numpy as jnp
import numpy as np
from jax import lax
from jax.experimental import pallas as pl
from jax.experimental.pallas import tpu as pltpu

GRID_W = 64
EPS = 1e-6
NA_HEADS = 8
NA_HEAD_DIM = 64
NA_WIN_R = 8
NA_WIN_C = 16
MLA_HEADS = 8
MLA_NOPE = 64
MLA_ROPE = 32
MLA_V = 64
ROPE_BASE = 10000.0
MEM_HEADS = 4
MEM_HEAD_DIM = 128
N_EXPERTS = 16
EC_FACTOR = 2
NA_W = NA_HEADS * NA_HEAD_DIM
MEM_W = MEM_HEADS * MEM_HEAD_DIM
LANES = 128
NEG = -1e30
VMEM_LIMIT = 56 * 1024 * 1024

F32 = jnp.float32
BF16 = jnp.bfloat16


def _dot(a, b):
    return jnp.dot(a, b, preferred_element_type=F32)


def _dot_nt(a, b):
    return lax.dot_general(a, b, (((1,), (1,)), ((), ())), preferred_element_type=F32)


def _rms(xf, g):
    return xf * lax.rsqrt(jnp.mean(xf * xf, axis=-1, keepdims=True) + EPS) * g


def _sigmoid(v):
    return 1.0 / (1.0 + jnp.exp(-v))


def _full(shape):
    return pl.BlockSpec(shape, lambda *_: (0,) * len(shape))


def _params(n_axes):
    return pltpu.CompilerParams(dimension_semantics=("arbitrary",) * n_axes,
                                vmem_limit_bytes=VMEM_LIMIT)


def _inproj_kernel(x_ref, gmix_ref, wna_ref, wcq_ref, wckv_ref, wkr_ref, wqm_ref, qn_ref, kvn_ref,
                   wuq_ref, wukv_ref, cos_ref, sin_ref,
                   na_ref, qmla_ref, kmla_ref, vmla_ref, qmem_ref, *, mla_scale, mem_scale):
    n = _rms(x_ref[...], gmix_ref[...]).astype(BF16)
    z_na = _dot(n, wna_ref[...])
    na_ref[:, :NA_W] = (z_na[:, :NA_W] * (NA_HEAD_DIM ** -0.5)).astype(BF16)
    na_ref[:, NA_W:] = z_na[:, NA_W:].astype(BF16)
    qmem_ref[...] = (_dot(n, wqm_ref[...]) * mem_scale).astype(BF16)

    cos = cos_ref[...]
    sin = sin_ref[...]
    zkr = _dot(n, wkr_ref[...])
    kpe = zkr[:, :LANES] * cos + zkr[:, LANES:] * sin

    cqn = _rms(_dot(n, wcq_ref[...]), qn_ref[...]).astype(BF16)
    zq = _dot(cqn, wuq_ref[...])
    hw = MLA_HEADS * LANES
    for h in range(MLA_HEADS):
        a = zq[:, h * LANES:(h + 1) * LANES]
        b = zq[:, hw + h * LANES:hw + (h + 1) * LANES]
        qmla_ref[:, h * LANES:(h + 1) * LANES] = ((a * cos + b * sin) * mla_scale).astype(BF16)

    ckvn = _rms(_dot(n, wckv_ref[...]), kvn_ref[...]).astype(BF16)
    zkv = _dot(ckvn, wukv_ref[...])
    for h in range(MLA_HEADS):
        kmla_ref[:, h * LANES:(h + 1) * LANES] = (zkv[:, h * LANES:(h + 1) * LANES] + kpe).astype(BF16)
    is_v = lax.broadcasted_iota(jnp.int32, (zkv.shape[0], LANES), 1) < MLA_V
    for h in range(MLA_HEADS):
        blk = zkv[:, hw + h * LANES:hw + (h + 1) * LANES]
        vmla_ref[:, h * LANES:(h + 1) * LANES] = jnp.where(is_v, blk, 1.0).astype(BF16)


def _inproj(x2, gmix, wna, wcq, wckv, wkr, wqm, qn, kvn, wuq, wukv, cos, sin, *, seq, tm):
    t, d = x2.shape
    nblk = seq // tm
    row = lambda w: pl.BlockSpec((tm, w), lambda i: (i, 0))
    rope = pl.BlockSpec((tm, LANES), lambda i: (i % nblk, 0))
    hw = MLA_HEADS * LANES
    kern = functools.partial(_inproj_kernel, mla_scale=(MLA_NOPE + MLA_ROPE) ** -0.5,
                             mem_scale=MEM_HEAD_DIM ** -0.5)
    return pl.pallas_call(
        kern,
        grid=(t // tm,),
        in_specs=[row(d), _full(gmix.shape), _full(wna.shape), _full(wcq.shape), _full(wckv.shape),
                  _full(wkr.shape), _full(wqm.shape), _full(qn.shape), _full(kvn.shape),
                  _full(wuq.shape), _full(wukv.shape), rope, rope],
        out_specs=[row(3 * NA_W), row(hw), row(hw), row(hw), row(MEM_W)],
        out_shape=[jax.ShapeDtypeStruct((t, 3 * NA_W), BF16),
                   jax.ShapeDtypeStruct((t, hw), BF16),
                   jax.ShapeDtypeStruct((t, hw), BF16),
                   jax.ShapeDtypeStruct((t, hw), BF16),
                   jax.ShapeDtypeStruct((t, MEM_W), BF16)],
        compiler_params=_params(1),
        name="inproj",
    )(x2, gmix, wna, wcq, wckv, wkr, wqm, qn, kvn, wuq, wukv, cos, sin)


NA_QROWS = NA_WIN_R // 2
NA_BAND = 12


def _na_kernel(q_ref, k_ref, v_ref, bias_ref, y_ref, *, rows):
    i = pl.program_id(1)
    us = jnp.clip(i * NA_QROWS - NA_WIN_R // 2, 0, rows - NA_BAND)
    start = pl.multiple_of(us * GRID_W, GRID_W)
    band = NA_BAND * GRID_W
    first = lax.broadcasted_iota(jnp.int32, (NA_QROWS * GRID_W, LANES), 1) < NA_HEAD_DIM
    kfirst = lax.broadcasted_iota(jnp.int32, (band, LANES), 1) < NA_HEAD_DIM
    for hp in range(NA_HEADS // 2):
        cols = slice(hp * LANES, (hp + 1) * LANES)
        qp = q_ref[:, cols]
        kp = k_ref[pl.ds(start, band), cols]
        vp = v_ref[pl.ds(start, band), cols]
        outs = []
        for s in range(2):
            mine = first if s == 0 else ~first
            qm = jnp.where(mine, qp, jnp.zeros_like(qp))
            lg = _dot_nt(qm, kp) + bias_ref[0, 2 * hp + s]
            p = jnp.exp((lg - jnp.max(lg, axis=1, keepdims=True)).astype(BF16))
            oa = _dot(p, jnp.where(kfirst if s == 0 else ~kfirst, vp, jnp.ones_like(vp)))
            other = (1 - s) * NA_HEAD_DIM
            outs.append(oa * (1.0 / oa[:, other:other + 1]))
        y_ref[:, cols] = jnp.where(first, outs[0], outs[1]).astype(BF16)


def _na_bias_table(rpb):
    q, bn, wr, wc = NA_QROWS, NA_BAND, NA_WIN_R, NA_WIN_C
    rho = np.arange(q)
    own = np.stack([0 * rho, rho, 0 * rho + (bn - wr)])
    dd = np.stack([rho, 0 * rho + wr // 2, rho + wr // 2])
    ip = np.arange(bn)[None, None, :] - own[:, :, None]
    rvalid = (ip >= 0) & (ip < wr)
    dr = ip - dd[:, :, None] + (wr - 1)
    rsel = (dr[..., None] == np.arange(2 * wr - 1)) & rvalid[..., None]
    c = np.arange(GRID_W)[:, None]
    j = np.arange(GRID_W)[None, :]
    cstart = np.clip(c - wc // 2, 0, GRID_W - wc)
    cvalid = (j >= cstart) & (j < cstart + wc)
    csel = ((j - c + wc - 1)[None] == np.arange(2 * wc - 1)[:, None, None]) & cvalid[None]
    tab = jnp.einsum("hrk,vpir,kcj->vhpcij", rpb.astype(F32), jnp.asarray(rsel, F32), jnp.asarray(csel, F32),
                     precision=lax.Precision.HIGHEST)
    valid = rvalid[:, None, :, None, :, None] & cvalid[None, None, None, :, None, :]
    tab = jnp.where(jnp.asarray(valid), tab, NEG)
    return tab.reshape(3, NA_HEADS, q * GRID_W, bn * GRID_W)


def _na_attention(na, bias, *, batch, seq):
    rows = seq // GRID_W
    nblk = rows // NA_QROWS
    assert rows % NA_QROWS == 0 and rows >= NA_BAND and NA_BAND >= NA_QROWS + NA_WIN_R - 1
    nq = NA_QROWS * GRID_W

    def bias_map(b, i):
        return (jnp.where(i == 0, 0, jnp.where(i == nblk - 1, 2, 1)), 0, 0, 0)

    return pl.pallas_call(
        functools.partial(_na_kernel, rows=rows),
        grid=(batch, nblk),
        in_specs=[pl.BlockSpec((nq, NA_W), lambda b, i: (b * nblk + i, 0)),
                  pl.BlockSpec((seq, NA_W), lambda b, i: (b, 1)),
                  pl.BlockSpec((seq, NA_W), lambda b, i: (b, 2)),
                  pl.BlockSpec((1, NA_HEADS, nq, NA_BAND * GRID_W), bias_map)],
        out_specs=pl.BlockSpec((nq, NA_W), lambda b, i: (b * nblk + i, 0)),
        out_shape=jax.ShapeDtypeStruct((batch * seq, NA_W), BF16),
        compiler_params=_params(2),
        name="na_attn",
    )(na, na, na, bias)


def _mla_kernel(q_ref, k_ref, v_ref, o_ref):
    tq = q_ref.shape[0]
    first = lax.broadcasted_iota(jnp.int32, (tq, LANES), 1) < MLA_V
    for pr in range(MLA_HEADS_PER_STEP // 2):
        outs = []
        for s in range(2):
            cols = slice((2 * pr + s) * LANES, (2 * pr + s + 1) * LANES)
            lg = _dot_nt(q_ref[:, cols], k_ref[:, cols])
            p = jnp.exp((lg - jnp.max(lg, axis=1, keepdims=True)).astype(BF16))
            oa = _dot(p, v_ref[:, cols])
            outs.append(oa * (1.0 / oa[:, MLA_V:MLA_V + 1]))
        o_ref[:, pr * LANES:(pr + 1) * LANES] = jnp.where(
            first, outs[0], pltpu.roll(outs[1], MLA_V, 1)).astype(BF16)


MLA_HEADS_PER_STEP = 4


def _mla_attention(q, k, v, *, batch, seq, tq):
    nq = seq // tq
    hps = MLA_HEADS_PER_STEP
    hp = MLA_HEADS // hps
    return pl.pallas_call(
        _mla_kernel,
        grid=(batch, hp, nq),
        in_specs=[pl.BlockSpec((tq, hps * LANES), lambda b, h, i: (b * nq + i, h)),
                  pl.BlockSpec((seq, hps * LANES), lambda b, h, i: (b, h)),
                  pl.BlockSpec((seq, hps * LANES), lambda b, h, i: (b, h))],
        out_specs=pl.BlockSpec((tq, hps * MLA_V), lambda b, h, i: (b * nq + i, h)),
        out_shape=jax.ShapeDtypeStruct((batch * seq, MLA_HEADS * MLA_V), BF16),
        compiler_params=_params(3),
        name="mla_attn",
    )(q, k, v)


def _memkv_kernel(mem_ref, g_ref, w_ref, kv_ref):
    kv_ref[...] = _dot(_rms(mem_ref[...], g_ref[...]).astype(BF16), w_ref[...]).astype(BF16)


def _memkv(mem2, g, w, *, batch, n_mem):
    d = mem2.shape[1]
    return pl.pallas_call(
        _memkv_kernel,
        grid=(batch,),
        in_specs=[pl.BlockSpec((n_mem, d), lambda b: (b, 0)), _full(g.shape), _full(w.shape)],
        out_specs=pl.BlockSpec((n_mem, 2 * MEM_W), lambda b: (b, 0)),
        out_shape=jax.ShapeDtypeStruct((batch * n_mem, 2 * MEM_W), BF16),
        compiler_params=_params(1),
        name="memkv",
    )(mem2, g, w)


def _merge_kernel(x_ref, yna_ref, ymla_ref, qmem_ref, kvm_ref, gmix_ref, wg_ref, wona_ref, womla_ref,
                  womem_ref, wout_ref, gffn_ref, wrh_ref, wrl_ref, br_ref, h1_ref, xn_ref, lg_ref):
    x = x_ref[...]
    d = x.shape[1]
    n = _rms(x, gmix_ref[...]).astype(BF16)

    ymem = []
    for h in range(MEM_HEADS):
        cols = slice(h * LANES, (h + 1) * LANES)
        lg = _dot_nt(qmem_ref[:, cols], kvm_ref[:, cols])
        p = jnp.exp(lg - jnp.max(lg, axis=1, keepdims=True))
        l = jnp.sum(p, axis=1, keepdims=True)
        vh = kvm_ref[:, MEM_W + h * LANES:MEM_W + (h + 1) * LANES]
        ymem.append((_dot(p.astype(BF16), vh) * (1.0 / l)).astype(BF16))
    ymem = jnp.concatenate(ymem, axis=1)

    merged = None
    for j, (y, w_ref) in enumerate(((yna_ref[...], wona_ref), (ymla_ref[...], womla_ref), (ymem, womem_ref))):
        gate = _sigmoid(_dot(n, wg_ref[:, j * d:(j + 1) * d]))
        term = gate * _dot(y, w_ref[...])
        merged = term if merged is None else merged + term
    h1 = x + _dot(merged.astype(BF16), wout_ref[...])
    h1_ref[...] = h1

    xn = _rms(h1, gffn_ref[...])
    hi = lax.bitcast_convert_type(lax.bitcast_convert_type(xn, jnp.int32) & jnp.int32(-65536), F32)
    xhi = hi.astype(BF16)
    xlo = (xn - hi).astype(BF16)
    wsum_lo = _dot(xhi, wrl_ref[...]) + _dot(xlo, wrh_ref[...]) + _dot(xlo, wrl_ref[...])
    lg_ref[...] = _dot(xhi, wrh_ref[...]) + wsum_lo + br_ref[...]
    xn_ref[...] = xn


def _merge(x2, yna, ymla, qmem, kvm, gmix, wg, wona, womla, womem, wout, gffn, wrh, wrl, br, *, seq, n_mem, tm):
    t, d = x2.shape
    nblk = seq // tm
    row = lambda w: pl.BlockSpec((tm, w), lambda i: (i, 0))
    return pl.pallas_call(
        _merge_kernel,
        grid=(t // tm,),
        in_specs=[row(d), row(NA_W), row(MLA_HEADS * MLA_V), row(MEM_W),
                  pl.BlockSpec((n_mem, 2 * MEM_W), lambda i: (i // nblk, 0)),
                  _full(gmix.shape), _full(wg.shape), _full(wona.shape), _full(womla.shape),
                  _full(womem.shape), _full(wout.shape), _full(gffn.shape), _full(wrh.shape),
                  _full(wrl.shape), _full(br.shape)],
        out_specs=[row(d), row(d), row(LANES)],
        out_shape=[jax.ShapeDtypeStruct((t, d), F32),
                   jax.ShapeDtypeStruct((t, d), F32),
                   jax.ShapeDtypeStruct((t, LANES), F32)],
        compiler_params=_params(1),
        name="merge",
    )(x2, yna, ymla, qmem, kvm, gmix, wg, wona, womla, womem, wout, gffn, wrh, wrl, br)


def _route_kernel(lg_ref, idx_ref, aff_ref, bnd_ref, cs_ref, *, cap, nsplit, tb):
    s = lg_ref.shape[0]
    lg = lg_ref[...]
    ex = jnp.exp(lg - jnp.max(lg, axis=1, keepdims=True))
    aff = ex / jnp.sum(ex, axis=1, keepdims=True)
    aff_ref[...] = aff

    aff_t = aff.T[:N_EXPERTS, :]

    def search(i, lo):
        cand = lo | lax.shift_left(jnp.int32(1), 30 - i)
        cnt = jnp.sum((aff_t >= lax.bitcast_convert_type(cand, F32)).astype(jnp.int32), axis=1, keepdims=True)
        return jnp.where(cnt >= cap, cand, lo)

    thr_col = lax.bitcast_convert_type(
        lax.fori_loop(0, 31, search, jnp.zeros((N_EXPERTS, 1), jnp.int32)), F32)
    thr_sq = jnp.concatenate([jnp.broadcast_to(thr_col, (N_EXPERTS, LANES)),
                              jnp.zeros((LANES - N_EXPERTS, LANES), F32)], axis=0)
    thr = thr_sq.T[0:1, :]
    gt = aff > thr
    eq = aff == thr
    need = cap - jnp.sum(gt.astype(jnp.int32), axis=0, keepdims=True)

    tri = (lax.broadcasted_iota(jnp.int32, (tb, tb), 0)
           >= lax.broadcasted_iota(jnp.int32, (tb, tb), 1)).astype(BF16)

    def cumsum_blocks(mask_of_block):
        carry = jnp.zeros((1, LANES), F32)
        for blk in range(s // tb):
            rows = slice(blk * tb, (blk + 1) * tb)
            c = _dot(tri, mask_of_block(rows).astype(BF16)) + carry
            cs_ref[rows, :] = c
            carry = c[tb - 1:tb, :]

    cumsum_blocks(lambda rows: eq[rows, :])
    eq_rank = cs_ref[...] - eq.astype(F32)
    sel = gt | (eq & (eq_rank < need.astype(F32)))
    cumsum_blocks(lambda rows: sel[rows, :])

    sq = s // nsplit
    sub = lax.broadcasted_iota(jnp.int32, (8, LANES), 0)
    bnd = jnp.zeros((8, LANES), jnp.int32)
    for k in range(1, nsplit + 1):
        bnd = jnp.where(sub == k, cs_ref[k * sq - 1:k * sq, :].astype(jnp.int32), bnd)
    bnd_ref[0] = bnd

    jrow = lax.broadcasted_iota(jnp.int32, (1, cap), 1).astype(F32)
    ones = jnp.ones((8, tb), BF16)
    for e in range(N_EXPERTS):
        cnt = jnp.zeros((8, cap), F32)
        for blk in range(s // tb):
            col = cs_ref[blk * tb:(blk + 1) * tb, e:e + 1]
            cnt = cnt + _dot(ones, jnp.where(col <= jrow, 1.0, 0.0).astype(BF16))
        idx_ref[0, e:e + 1, :] = cnt[0:1, :].astype(jnp.int32)


def _route(lg, *, batch, seq, cap, nsplit):
    tb = min(512, seq)
    return pl.pallas_call(
        functools.partial(_route_kernel, cap=cap, nsplit=nsplit, tb=tb),
        grid=(batch,),
        in_specs=[pl.BlockSpec((seq, LANES), lambda b: (b, 0))],
        out_specs=[pl.BlockSpec((1, N_EXPERTS, cap), lambda b: (b, 0, 0)),
                   pl.BlockSpec((seq, LANES), lambda b: (b, 0)),
                   pl.BlockSpec((1, 8, LANES), lambda b: (b, 0, 0))],
        out_shape=[jax.ShapeDtypeStruct((batch, N_EXPERTS, cap), jnp.int32),
                   jax.ShapeDtypeStruct((batch * seq, LANES), F32),
                   jax.ShapeDtypeStruct((batch, 8, LANES), jnp.int32)],
        scratch_shapes=[pltpu.VMEM((seq, LANES), F32)],
        compiler_params=_params(1),
        name="route",
    )(lg)


EXPERT_FF_CHUNKS = 4


def _expert_kernel(idx_ref, xn_ref, aff_ref, wg_ref, wu_ref, wd_ref, yg_ref, xa_ref, ga_ref, xb_ref, gb_ref,
                   *, cap):
    e = pl.program_id(1)
    last = pl.num_programs(1) - 1
    ff = wg_ref.shape[2]
    fc = ff // EXPERT_FF_CHUNKS
    rc = cap // EXPERT_FF_CHUNKS

    def gather_row(ex, i, xs_ref, gs_ref):
        t = idx_ref[ex, 0, i]
        xs_ref[pl.ds(i, 1), :] = xn_ref[pl.ds(t, 1), :]
        gs_ref[pl.ds(i, 1), :] = aff_ref[pl.ds(t, 1), :]

    @pl.when(e == 0)
    def _():
        def body(i, c):
            gather_row(0, i, xa_ref, ga_ref)
            return c
        lax.fori_loop(0, cap, body, 0, unroll=8)

    def compute(cur_x, cur_g, nxt_x, nxt_g):
        nxt = jnp.minimum(e + 1, last)
        xs = cur_x[...].astype(BF16)
        y = None
        for c in range(EXPERT_FF_CHUNKS):
            fcols = slice(c * fc, (c + 1) * fc)
            hg = _dot(xs, wg_ref[0, :, fcols])
            hu = _dot(xs, wu_ref[0, :, fcols])
            act = (hg * _sigmoid(hg) * hu).astype(BF16)
            part = _dot(act, wd_ref[0, fcols, :])
            y = part if y is None else y + part
            for i in range(c * rc, (c + 1) * rc):
                gather_row(nxt, i, nxt_x, nxt_g)
        lane = lax.broadcasted_iota(jnp.int32, cur_g.shape, 1)
        g = jnp.sum(jnp.where(lane == e, cur_g[...], 0.0), axis=1, keepdims=True)
        yg_ref[0] = y * g

    @pl.when(e % 2 == 0)
    def _():
        compute(xa_ref, ga_ref, xb_ref, gb_ref)

    @pl.when(e % 2 == 1)
    def _():
        compute(xb_ref, gb_ref, xa_ref, ga_ref)


def _experts(idx3, xn, aff, wg, wu, wd, *, batch, seq, cap):
    d = wg.shape[1]
    ff = wg.shape[2]
    ne = N_EXPERTS
    once = pl.Buffered(1)
    return pl.pallas_call(
        functools.partial(_expert_kernel, cap=cap),
        grid=(batch, ne),
        in_specs=[pl.BlockSpec((ne, 1, cap), lambda b, e: (b, 0, 0), memory_space=pltpu.SMEM),
                  pl.BlockSpec((seq, d), lambda b, e: (b, 0), pipeline_mode=once),
                  pl.BlockSpec((seq, LANES), lambda b, e: (b, 0), pipeline_mode=once),
                  pl.BlockSpec((1, d, ff), lambda b, e: (e, 0, 0)),
                  pl.BlockSpec((1, d, ff), lambda b, e: (e, 0, 0)),
                  pl.BlockSpec((1, ff, d), lambda b, e: (e, 0, 0))],
        out_specs=pl.BlockSpec((1, cap, d), lambda b, e: (b * ne + e, 0, 0)),
        out_shape=jax.ShapeDtypeStruct((batch * ne, cap, d), F32),
        scratch_shapes=[pltpu.VMEM((cap, d), F32), pltpu.VMEM((cap, LANES), F32),
                        pltpu.VMEM((cap, d), F32), pltpu.VMEM((cap, LANES), F32)],
        compiler_params=_params(2),
        name="experts",
    )(idx3, xn, aff, wg, wu, wd)


def _combine_kernel(idx_ref, bnd_ref, h1_ref, yg_ref, gfin_ref, out_ref, *, sq):
    q = pl.program_id(1)
    e = pl.program_id(2)

    @pl.when(e == 0)
    def _():
        out_ref[...] = h1_ref[...]

    base = q * sq

    def scatter(i, c):
        t = idx_ref[0, 0, i] - base
        out_ref[pl.ds(t, 1), :] += yg_ref[0, pl.ds(i, 1), :]
        return c

    def scatter4(k, c):
        i = lo + 4 * k
        ts = [idx_ref[0, 0, i + u] - base for u in range(4)]
        new = [out_ref[pl.ds(ts[u], 1), :] + yg_ref[0, pl.ds(i + u, 1), :] for u in range(4)]
        for u in range(4):
            out_ref[pl.ds(ts[u], 1), :] = new[u]
        return c

    lo = bnd_ref[0, q, e]
    hi = bnd_ref[0, q + 1, e]
    quads = lax.shift_right_logical(hi - lo, 2)
    lax.fori_loop(0, quads, scatter4, 0)
    lax.fori_loop(lo + 4 * quads, hi, scatter, 0)

    @pl.when(e == pl.num_programs(2) - 1)
    def _():
        out_ref[...] = _rms(out_ref[...], gfin_ref[...])


def _combine(idx3, bnd, h1, yg, gfin, *, batch, seq, cap, nsplit):
    d = h1.shape[1]
    ne = N_EXPERTS
    sq = seq // nsplit
    return pl.pallas_call(
        functools.partial(_combine_kernel, sq=sq),
        grid=(batch, nsplit, ne),
        in_specs=[pl.BlockSpec((1, 1, cap), lambda b, q, e: (b * ne + e, 0, 0), memory_space=pltpu.SMEM),
                  pl.BlockSpec((1, 8, LANES), lambda b, q, e: (b, 0, 0), memory_space=pltpu.SMEM),
                  pl.BlockSpec((sq, d), lambda b, q, e: (b * nsplit + q, 0)),
                  pl.BlockSpec((1, cap, d), lambda b, q, e: (b * ne + e, 0, 0)),
                  _full(gfin.shape)],
        out_specs=pl.BlockSpec((sq, d), lambda b, q, e: (b * nsplit + q, 0)),
        out_shape=jax.ShapeDtypeStruct((batch * seq, d), F32),
        compiler_params=_params(3),
        name="combine",
    )(idx3, bnd, h1, yg, gfin)


def _rope_tables(seq):
    t = jnp.arange(seq)
    half = MLA_ROPE // 2
    inv = ROPE_BASE ** (-jnp.arange(0, half, 2, dtype=F32) / half)
    ang_r = (t // GRID_W).astype(F32)[:, None] * inv[None, :]
    ang_c = (t % GRID_W).astype(F32)[:, None] * inv[None, :]
    cos = jnp.concatenate([jnp.cos(ang_r), jnp.cos(ang_r), jnp.cos(ang_c), jnp.cos(ang_c)], axis=1)
    sin = jnp.concatenate([-jnp.sin(ang_r), jnp.sin(ang_r), -jnp.sin(ang_c), jnp.sin(ang_c)], axis=1)
    pad = LANES - MLA_NOPE - MLA_ROPE
    cos = jnp.concatenate([jnp.ones((seq, MLA_NOPE), F32), cos, jnp.zeros((seq, pad), F32)], axis=1)
    sin = jnp.concatenate([jnp.zeros((seq, MLA_NOPE), F32), sin, jnp.zeros((seq, pad), F32)], axis=1)
    return cos, sin


_q = MLA_ROPE // 4
ROPE_SWAP = np.concatenate([np.arange(_q, 2 * _q), np.arange(0, _q),
                            np.arange(3 * _q, 4 * _q), np.arange(2 * _q, 3 * _q)])


def kernel(x, mem, g_mix, w_in, na_rpb, mla_q_norm, w_mla_uq, mla_kv_norm, w_mla_ukv, g_mem, w_mem_kv,
           w_o_na, w_o_mla, w_o_mem, w_out, g_ffn, w_router, b_router, w_gate, w_up, w_down, g_final):
    batch, seq, d = x.shape
    n_mem = mem.shape[1]
    depth = g_mix.shape[0]
    q_rank = mla_q_norm.shape[1]
    kv_rank = mla_kv_norm.shape[1]
    cap = EC_FACTOR * seq // N_EXPERTS
    nsplit = 2
    tm = 512
    pad = LANES - MLA_NOPE - MLA_ROPE
    cos, sin = _rope_tables(seq)

    assert depth == 1
    h = x.reshape(batch * seq, d)
    for l in range(depth):
        w = w_in[l]
        o = np.cumsum([0, NA_W, NA_W, NA_W, q_rank, kv_rank, MLA_ROPE, MEM_W, 3 * d])
        wna = w[:, o[0]:o[3]].astype(BF16)
        wcq = w[:, o[3]:o[4]].astype(BF16)
        wckv = w[:, o[4]:o[5]].astype(BF16)
        wkr_raw = w[:, o[5]:o[6]]
        wqm = w[:, o[6]:o[7]].astype(BF16)
        wgate_cols = w[:, o[7]:o[8]].astype(BF16)
        z = lambda *s: jnp.zeros(s, F32)
        wkr = jnp.concatenate([z(d, MLA_NOPE), wkr_raw, z(d, pad),
                               z(d, MLA_NOPE), wkr_raw[:, ROPE_SWAP], z(d, pad)], axis=1).astype(BF16)
        uq = w_mla_uq[l].reshape(q_rank, MLA_HEADS, MLA_NOPE + MLA_ROPE)
        uq_a = jnp.concatenate([uq, z(q_rank, MLA_HEADS, pad)], axis=2)
        uq_b = jnp.concatenate([z(q_rank, MLA_HEADS, MLA_NOPE), uq[:, :, MLA_NOPE:][:, :, ROPE_SWAP],
                                z(q_rank, MLA_HEADS, pad)], axis=2)
        wuq = jnp.concatenate([uq_a.reshape(q_rank, -1), uq_b.reshape(q_rank, -1)], axis=1).astype(BF16)
        ukv = w_mla_ukv[l].reshape(kv_rank, MLA_HEADS, MLA_NOPE + MLA_V)
        uk = jnp.concatenate([ukv[:, :, :MLA_NOPE], z(kv_rank, MLA_HEADS, LANES - MLA_NOPE)], axis=2)
        uv = jnp.concatenate([ukv[:, :, MLA_NOPE:], z(kv_rank, MLA_HEADS, LANES - MLA_V)], axis=2)
        wukv = jnp.concatenate([uk.reshape(kv_rank, -1), uv.reshape(kv_rank, -1)], axis=1).astype(BF16)
        wr = jnp.concatenate([w_router[l].astype(F32), z(d, LANES - N_EXPERTS)], axis=1)
        wr_hi = lax.bitcast_convert_type(lax.bitcast_convert_type(wr, jnp.int32) & jnp.int32(-65536), F32)
        wrh = wr_hi.astype(BF16)
        wrl = (wr - wr_hi).astype(BF16)
        br = jnp.concatenate([b_router[l].astype(F32), jnp.full((LANES - N_EXPERTS,), NEG, F32)])[None, :]

        na, qmla, kmla, vmla, qmem = _inproj(
            h, g_mix[l][None, :], wna, wcq, wckv, wkr, wqm, mla_q_norm[l][None, :], mla_kv_norm[l][None, :],
            wuq, wukv, cos, sin, seq=seq, tm=tm)
        y_na = _na_attention(na, _na_bias_table(na_rpb[l]), batch=batch, seq=seq)
        y_mla = _mla_attention(qmla, kmla, vmla, batch=batch, seq=seq, tq=256)
        kvm = _memkv(mem.reshape(batch * n_mem, d), g_mem[l][None, :], w_mem_kv[l].astype(BF16),
                     batch=batch, n_mem=n_mem)
        h1, xn, lg = _merge(h, y_na, y_mla, qmem, kvm, g_mix[l][None, :], wgate_cols,
                             w_o_na[l].astype(BF16), w_o_mla[l].astype(BF16), w_o_mem[l].astype(BF16),
                             w_out[l].astype(BF16), g_ffn[l][None, :], wrh, wrl, br,
                             seq=seq, n_mem=n_mem, tm=256)
        idx, aff, bnd = _route(lg, batch=batch, seq=seq, cap=cap, nsplit=nsplit)
        idx3 = idx.reshape(batch * N_EXPERTS, 1, cap)
        yg = _experts(idx3, xn, aff, w_gate[l].astype(BF16), w_up[l].astype(BF16), w_down[l].astype(BF16),
                      batch=batch, seq=seq, cap=cap)
        h = _combine(idx3, bnd, h1, yg, g_final[None, :], batch=batch, seq=seq, cap=cap, nsplit=nsplit)
    return h.reshape(batch, seq, d)
```

```python
import functools

import jax
import jax.numpy as jnp
import numpy as np
from jax import lax
from jax.experimental import pallas as pl
from jax.experimental.pallas import tpu as pltpu

GRID_W = 64
EPS = 1e-6
NA_HEADS = 8
NA_HEAD_DIM = 64
NA_WIN_R = 8
NA_WIN_C = 16
MLA_HEADS = 8
MLA_NOPE = 64
MLA_ROPE = 32
MLA_V = 64
ROPE_BASE = 10000.0
MEM_HEADS = 4
MEM_HEAD_DIM = 128
N_EXPERTS = 16
EC_FACTOR = 2
NA_W = NA_HEADS * NA_HEAD_DIM
MEM_W = MEM_HEADS * MEM_HEAD_DIM
LANES = 128
NEG = -1e30
VMEM_LIMIT = 56 * 1024 * 1024

F32 = jnp.float32
BF16 = jnp.bfloat16


def _dot(a, b):
    return jnp.dot(a, b, preferred_element_type=F32)


def _dot_nt(a, b):
    return lax.dot_general(a, b, (((1,), (1,)), ((), ())), preferred_element_type=F32)


def _rms(xf, g):
    return xf * lax.rsqrt(jnp.mean(xf * xf, axis=-1, keepdims=True) + EPS) * g


def _sigmoid(v):
    return 1.0 / (1.0 + jnp.exp(-v))


def _full(shape):
    return pl.BlockSpec(shape, lambda *_: (0,) * len(shape))


def _params(n_axes):
    return pltpu.CompilerParams(dimension_semantics=("arbitrary",) * n_axes,
                                vmem_limit_bytes=VMEM_LIMIT)


def _inproj_kernel(x_ref, gmix_ref, wna_ref, wcq_ref, wckv_ref, wkr_ref, wqm_ref, qn_ref, kvn_ref,
                   wuq_ref, wukv_ref, cos_ref, sin_ref,
                   na_ref, qmla_ref, kmla_ref, vmla_ref, qmem_ref, *, mla_scale, mem_scale):
    n = _rms(x_ref[...], gmix_ref[...]).astype(BF16)
    z_na = _dot(n, wna_ref[...])
    na_ref[:, :NA_W] = (z_na[:, :NA_W] * (NA_HEAD_DIM ** -0.5)).astype(BF16)
    na_ref[:, NA_W:] = z_na[:, NA_W:].astype(BF16)
    qmem_ref[...] = (_dot(n, wqm_ref[...]) * mem_scale).astype(BF16)

    cos = cos_ref[...]
    sin = sin_ref[...]
    zkr = _dot(n, wkr_ref[...])
    kpe = zkr[:, :LANES] * cos + zkr[:, LANES:] * sin

    cqn = _rms(_dot(n, wcq_ref[...]), qn_ref[...]).astype(BF16)
    zq = _dot(cqn, wuq_ref[...])
    hw = MLA_HEADS * LANES
    for h in range(MLA_HEADS):
        a = zq[:, h * LANES:(h + 1) * LANES]
        b = zq[:, hw + h * LANES:hw + (h + 1) * LANES]
        qmla_ref[:, h * LANES:(h + 1) * LANES] = ((a * cos + b * sin) * mla_scale).astype(BF16)

    ckvn = _rms(_dot(n, wckv_ref[...]), kvn_ref[...]).astype(BF16)
    zkv = _dot(ckvn, wukv_ref[...])
    for h in range(MLA_HEADS):
        kmla_ref[:, h * LANES:(h + 1) * LANES] = (zkv[:, h * LANES:(h + 1) * LANES] + kpe).astype(BF16)
    is_v = lax.broadcasted_iota(jnp.int32, (zkv.shape[0], LANES), 1) < MLA_V
    for h in range(MLA_HEADS):
        blk = zkv[:, hw + h * LANES:hw + (h + 1) * LANES]
        vmla_ref[:, h * LANES:(h + 1) * LANES] = jnp.where(is_v, blk, 1.0).astype(BF16)


def _inproj(x2, gmix, wna, wcq, wckv, wkr, wqm, qn, kvn, wuq, wukv, cos, sin, *, seq, tm):
    t, d = x2.shape
    nblk = seq // tm
    row = lambda w: pl.BlockSpec((tm, w), lambda i: (i, 0))
    rope = pl.BlockSpec((tm, LANES), lambda i: (i % nblk, 0))
    hw = MLA_HEADS * LANES
    kern = functools.partial(_inproj_kernel, mla_scale=(MLA_NOPE + MLA_ROPE) ** -0.5,
                             mem_scale=MEM_HEAD_DIM ** -0.5)
    return pl.pallas_call(
        kern,
        grid=(t // tm,),
        in_specs=[row(d), _full(gmix.shape), _full(wna.shape), _full(wcq.shape), _full(wckv.shape),
                  _full(wkr.shape), _full(wqm.shape), _full(qn.shape), _full(kvn.shape),
                  _full(wuq.shape), _full(wukv.shape), rope, rope],
        out_specs=[row(3 * NA_W), row(hw), row(hw), row(hw), row(MEM_W)],
        out_shape=[jax.ShapeDtypeStruct((t, 3 * NA_W), BF16),
                   jax.ShapeDtypeStruct((t, hw), BF16),
                   jax.ShapeDtypeStruct((t, hw), BF16),
                   jax.ShapeDtypeStruct((t, hw), BF16),
                   jax.ShapeDtypeStruct((t, MEM_W), BF16)],
        compiler_params=_params(1),
        name="inproj",
    )(x2, gmix, wna, wcq, wckv, wkr, wqm, qn, kvn, wuq, wukv, cos, sin)


NA_QROWS = NA_WIN_R // 2
NA_BAND = 12


def _na_kernel(q_ref, k_ref, v_ref, bias_ref, y_ref, *, rows):
    i = pl.program_id(1)
    us = jnp.clip(i * NA_QROWS - NA_WIN_R // 2, 0, rows - NA_BAND)
    start = pl.multiple_of(us * GRID_W, GRID_W)
    band = NA_BAND * GRID_W
    first = lax.broadcasted_iota(jnp.int32, (NA_QROWS * GRID_W, LANES), 1) < NA_HEAD_DIM
    kfirst = lax.broadcasted_iota(jnp.int32, (band, LANES), 1) < NA_HEAD_DIM
    for hp in range(NA_HEADS // 2):
        cols = slice(hp * LANES, (hp + 1) * LANES)
        qp = q_ref[:, cols]
        kp = k_ref[pl.ds(start, band), cols]
        vp = v_ref[pl.ds(start, band), cols]
        outs = []
        for s in range(2):
            mine = first if s == 0 else ~first
            qm = jnp.where(mine, qp, jnp.zeros_like(qp))
            lg = _dot_nt(qm, kp) + bias_ref[0, 2 * hp + s]
            p = jnp.exp((lg - jnp.max(lg, axis=1, keepdims=True)).astype(BF16))
            oa = _dot(p, jnp.where(kfirst if s == 0 else ~kfirst, vp, jnp.ones_like(vp)))
            other = (1 - s) * NA_HEAD_DIM
            outs.append(oa * (1.0 / oa[:, other:other + 1]))
        y_ref[:, cols] = jnp.where(first, outs[0], outs[1]).astype(BF16)


def _na_bias_table(rpb):
    q, bn, wr, wc = NA_QROWS, NA_BAND, NA_WIN_R, NA_WIN_C
    rho = np.arange(q)
    own = np.stack([0 * rho, rho, 0 * rho + (bn - wr)])
    dd = np.stack([rho, 0 * rho + wr // 2, rho + wr // 2])
    ip = np.arange(bn)[None, None, :] - own[:, :, None]
    rvalid = (ip >= 0) & (ip < wr)
    dr = ip - dd[:, :, None] + (wr - 1)
    rsel = (dr[..., None] == np.arange(2 * wr - 1)) & rvalid[..., None]
    c = np.arange(GRID_W)[:, None]
    j = np.arange(GRID_W)[None, :]
    cstart = np.clip(c - wc // 2, 0, GRID_W - wc)
    cvalid = (j >= cstart) & (j < cstart + wc)
    csel = ((j - c + wc - 1)[None] == np.arange(2 * wc - 1)[:, None, None]) & cvalid[None]
    tab = jnp.einsum("hrk,vpir,kcj->vhpcij", rpb.astype(F32), jnp.asarray(rsel, F32), jnp.asarray(csel, F32),
                     precision=lax.Precision.HIGHEST)
    valid = rvalid[:, None, :, None, :, None] & cvalid[None, None, None, :, None, :]
    tab = jnp.where(jnp.asarray(valid), tab, NEG)
    return tab.reshape(3, NA_HEADS, q * GRID_W, bn * GRID_W)


def _na_attention(na, bias, *, batch, seq):
    rows = seq // GRID_W
    nblk = rows // NA_QROWS
    assert rows % NA_QROWS == 0 and rows >= NA_BAND and NA_BAND >= NA_QROWS + NA_WIN_R - 1
    nq = NA_QROWS * GRID_W

    def bias_map(b, i):
        return (jnp.where(i == 0, 0, jnp.where(i == nblk - 1, 2, 1)), 0, 0, 0)

    return pl.pallas_call(
        functools.partial(_na_kernel, rows=rows),
        grid=(batch, nblk),
        in_specs=[pl.BlockSpec((nq, NA_W), lambda b, i: (b * nblk + i, 0)),
                  pl.BlockSpec((seq, NA_W), lambda b, i: (b, 1)),
                  pl.BlockSpec((seq, NA_W), lambda b, i: (b, 2)),
                  pl.BlockSpec((1, NA_HEADS, nq, NA_BAND * GRID_W), bias_map)],
        out_specs=pl.BlockSpec((nq, NA_W), lambda b, i: (b * nblk + i, 0)),
        out_shape=jax.ShapeDtypeStruct((batch * seq, NA_W), BF16),
        compiler_params=_params(2),
        name="na_attn",
    )(na, na, na, bias)


def _mla_kernel(q_ref, k_ref, v_ref, o_ref):
    tq = q_ref.shape[0]
    first = lax.broadcasted_iota(jnp.int32, (tq, LANES), 1) < MLA_V
    for pr in range(MLA_HEADS_PER_STEP // 2):
        outs = []
        for s in range(2):
            cols = slice((2 * pr + s) * LANES, (2 * pr + s + 1) * LANES)
            lg = _dot_nt(q_ref[:, cols], k_ref[:, cols])
            p = jnp.exp((lg - jnp.max(lg, axis=1, keepdims=True)).astype(BF16))
            oa = _dot(p, v_ref[:, cols])
            outs.append(oa * (1.0 / oa[:, MLA_V:MLA_V + 1]))
        o_ref[:, pr * LANES:(pr + 1) * LANES] = jnp.where(
            first, outs[0], pltpu.roll(outs[1], MLA_V, 1)).astype(BF16)


MLA_HEADS_PER_STEP = 4


def _mla_attention(q, k, v, *, batch, seq, tq):
    nq = seq // tq
    hps = MLA_HEADS_PER_STEP
    hp = MLA_HEADS // hps
    return pl.pallas_call(
        _mla_kernel,
        grid=(batch, hp, nq),
        in_specs=[pl.BlockSpec((tq, hps * LANES), lambda b, h, i: (b * nq + i, h)),
                  pl.BlockSpec((seq, hps * LANES), lambda b, h, i: (b, h)),
                  pl.BlockSpec((seq, hps * LANES), lambda b, h, i: (b, h))],
        out_specs=pl.BlockSpec((tq, hps * MLA_V), lambda b, h, i: (b * nq + i, h)),
        out_shape=jax.ShapeDtypeStruct((batch * seq, MLA_HEADS * MLA_V), BF16),
        compiler_params=_params(3),
        name="mla_attn",
    )(q, k, v)


def _memkv_kernel(mem_ref, g_ref, w_ref, kv_ref):
    kv_ref[...] = _dot(_rms(mem_ref[...], g_ref[...]).astype(BF16), w_ref[...]).astype(BF16)


def _memkv(mem2, g, w, *, batch, n_mem):
    d = mem2.shape[1]
    return pl.pallas_call(
        _memkv_kernel,
        grid=(batch,),
        in_specs=[pl.BlockSpec((n_mem, d), lambda b: (b, 0)), _full(g.shape), _full(w.shape)],
        out_specs=pl.BlockSpec((n_mem, 2 * MEM_W), lambda b: (b, 0)),
        out_shape=jax.ShapeDtypeStruct((batch * n_mem, 2 * MEM_W), BF16),
        compiler_params=_params(1),
        name="memkv",
    )(mem2, g, w)


def _merge_kernel(x_ref, yna_ref, ymla_ref, qmem_ref, kvm_ref, gmix_ref, wg_ref, wona_ref, womla_ref,
                  womem_ref, wout_ref, gffn_ref, wrh_ref, wrl_ref, br_ref, h1_ref, xn_ref, lg_ref):
    x = x_ref[...]
    d = x.shape[1]
    n = _rms(x, gmix_ref[...]).astype(BF16)

    ymem = []
    for h in range(MEM_HEADS):
        cols = slice(h * LANES, (h + 1) * LANES)
        lg = _dot_nt(qmem_ref[:, cols], kvm_ref[:, cols])
        p = jnp.exp(lg - jnp.max(lg, axis=1, keepdims=True))
        l = jnp.sum(p, axis=1, keepdims=True)
        vh = kvm_ref[:, MEM_W + h * LANES:MEM_W + (h + 1) * LANES]
        ymem.append((_dot(p.astype(BF16), vh) * (1.0 / l)).astype(BF16))
    ymem = jnp.concatenate(ymem, axis=1)

    merged = None
    for j, (y, w_ref) in enumerate(((yna_ref[...], wona_ref), (ymla_ref[...], womla_ref), (ymem, womem_ref))):
        gate = _sigmoid(_dot(n, wg_ref[:, j * d:(j + 1) * d]))
        term = gate * _dot(y, w_ref[...])
        merged = term if merged is None else merged + term
    h1 = x + _dot(merged.astype(BF16), wout_ref[...])
    h1_ref[...] = h1

    xn = _rms(h1, gffn_ref[...])
    hi = lax.bitcast_convert_type(lax.bitcast_convert_type(xn, jnp.int32) & jnp.int32(-65536), F32)
    xhi = hi.astype(BF16)
    xlo = (xn - hi).astype(BF16)
    wsum_lo = _dot(xhi, wrl_ref[...]) + _dot(xlo, wrh_ref[...]) + _dot(xlo, wrl_ref[...])
    lg_ref[...] = _dot(xhi, wrh_ref[...]) + wsum_lo + br_ref[...]
    xn_ref[...] = xn


def _merge(x2, yna, ymla, qmem, kvm, gmix, wg, wona, womla, womem, wout, gffn, wrh, wrl, br, *, seq, n_mem, tm):
    t, d = x2.shape
    nblk = seq // tm
    row = lambda w: pl.BlockSpec((tm, w), lambda i: (i, 0))
    return pl.pallas_call(
        _merge_kernel,
        grid=(t // tm,),
        in_specs=[row(d), row(NA_W), row(MLA_HEADS * MLA_V), row(MEM_W),
                  pl.BlockSpec((n_mem, 2 * MEM_W), lambda i: (i // nblk, 0)),
                  _full(gmix.shape), _full(wg.shape), _full(wona.shape), _full(womla.shape),
                  _full(womem.shape), _full(wout.shape), _full(gffn.shape), _full(wrh.shape),
                  _full(wrl.shape), _full(br.shape)],
        out_specs=[row(d), row(d), row(LANES)],
        out_shape=[jax.ShapeDtypeStruct((t, d), F32),
                   jax.ShapeDtypeStruct((t, d), F32),
                   jax.ShapeDtypeStruct((t, LANES), F32)],
        compiler_params=_params(1),
        name="merge",
    )(x2, yna, ymla, qmem, kvm, gmix, wg, wona, womla, womem, wout, gffn, wrh, wrl, br)


def _route_kernel(lg_ref, idx_ref, aff_ref, bnd_ref, cs_ref, *, cap, nsplit, tb):
    s = lg_ref.shape[0]
    lg = lg_ref[...]
    ex = jnp.exp(lg - jnp.max(lg, axis=1, keepdims=True))
    aff = ex / jnp.sum(ex, axis=1, keepdims=True)
    aff_ref[...] = aff

    aff_t = aff.T[:N_EXPERTS, :]

    def search(i, lo):
        cand = lo | lax.shift_left(jnp.int32(1), 30 - i)
        cnt = jnp.sum((aff_t >= lax.bitcast_convert_type(cand, F32)).astype(jnp.int32), axis=1, keepdims=True)
        return jnp.where(cnt >= cap, cand, lo)

    thr_col = lax.bitcast_convert_type(
        lax.fori_loop(0, 31, search, jnp.zeros((N_EXPERTS, 1), jnp.int32)), F32)
    thr_sq = jnp.concatenate([jnp.broadcast_to(thr_col, (N_EXPERTS, LANES)),
                              jnp.zeros((LANES - N_EXPERTS, LANES), F32)], axis=0)
    thr = thr_sq.T[0:1, :]
    gt = aff > thr
    eq = aff == thr
    need = cap - jnp.sum(gt.astype(jnp.int32), axis=0, keepdims=True)

    tri = (lax.broadcasted_iota(jnp.int32, (tb, tb), 0)
           >= lax.broadcasted_iota(jnp.int32, (tb, tb), 1)).astype(BF16)

    def cumsum_blocks(mask_of_block):
        carry = jnp.zeros((1, LANES), F32)
        for blk in range(s // tb):
            rows = slice(blk * tb, (blk + 1) * tb)
            c = _dot(tri, mask_of_block(rows).astype(BF16)) + carry
            cs_ref[rows, :] = c
            carry = c[tb - 1:tb, :]

    cumsum_blocks(lambda rows: eq[rows, :])
    eq_rank = cs_ref[...] - eq.astype(F32)
    sel = gt | (eq & (eq_rank < need.astype(F32)))
    cumsum_blocks(lambda rows: sel[rows, :])

    sq = s // nsplit
    sub = lax.broadcasted_iota(jnp.int32, (8, LANES), 0)
    bnd = jnp.zeros((8, LANES), jnp.int32)
    for k in range(1, nsplit + 1):
        bnd = jnp.where(sub == k, cs_ref[k * sq - 1:k * sq, :].astype(jnp.int32), bnd)
    bnd_ref[0] = bnd

    jrow = lax.broadcasted_iota(jnp.int32, (1, cap), 1).astype(F32)
    ones = jnp.ones((8, tb), BF16)
    for e in range(N_EXPERTS):
        cnt = jnp.zeros((8, cap), F32)
        for blk in range(s // tb):
            col = cs_ref[blk * tb:(blk + 1) * tb, e:e + 1]
            cnt = cnt + _dot(ones, jnp.where(col <= jrow, 1.0, 0.0).astype(BF16))
        idx_ref[0, e:e + 1, :] = cnt[0:1, :].astype(jnp.int32)


def _route(lg, *, batch, seq, cap, nsplit):
    tb = min(512, seq)
    return pl.pallas_call(
        functools.partial(_route_kernel, cap=cap, nsplit=nsplit, tb=tb),
        grid=(batch,),
        in_specs=[pl.BlockSpec((seq, LANES), lambda b: (b, 0))],
        out_specs=[pl.BlockSpec((1, N_EXPERTS, cap), lambda b: (b, 0, 0)),
                   pl.BlockSpec((seq, LANES), lambda b: (b, 0)),
                   pl.BlockSpec((1, 8, LANES), lambda b: (b, 0, 0))],
        out_shape=[jax.ShapeDtypeStruct((batch, N_EXPERTS, cap), jnp.int32),
                   jax.ShapeDtypeStruct((batch * seq, LANES), F32),
                   jax.ShapeDtypeStruct((batch, 8, LANES), jnp.int32)],
        scratch_shapes=[pltpu.VMEM((seq, LANES), F32)],
        compiler_params=_params(1),
        name="route",
    )(lg)


EXPERT_FF_CHUNKS = 4


def _expert_kernel(idx_ref, xn_ref, aff_ref, wg_ref, wu_ref, wd_ref, yg_ref, xa_ref, ga_ref, xb_ref, gb_ref,
                   *, cap):
    e = pl.program_id(1)
    last = pl.num_programs(1) - 1
    ff = wg_ref.shape[2]
    fc = ff // EXPERT_FF_CHUNKS
    rc = cap // EXPERT_FF_CHUNKS

    def gather_row(ex, i, xs_ref, gs_ref):
        t = idx_ref[ex, 0, i]
        xs_ref[pl.ds(i, 1), :] = xn_ref[pl.ds(t, 1), :]
        gs_ref[pl.ds(i, 1), :] = aff_ref[pl.ds(t, 1), :]

    @pl.when(e == 0)
    def _():
        def body(i, c):
            gather_row(0, i, xa_ref, ga_ref)
            return c
        lax.fori_loop(0, cap, body, 0, unroll=8)

    def compute(cur_x, cur_g, nxt_x, nxt_g):
        nxt = jnp.minimum(e + 1, last)
        xs = cur_x[...].astype(BF16)
        y = None
        for c in range(EXPERT_FF_CHUNKS):
            fcols = slice(c * fc, (c + 1) * fc)
            hg = _dot(xs, wg_ref[0, :, fcols])
            hu = _dot(xs, wu_ref[0, :, fcols])
            act = (hg * _sigmoid(hg) * hu).astype(BF16)
            part = _dot(act, wd_ref[0, fcols, :])
            y = part if y is None else y + part
            for i in range(c * rc, (c + 1) * rc):
                gather_row(nxt, i, nxt_x, nxt_g)
        lane = lax.broadcasted_iota(jnp.int32, cur_g.shape, 1)
        g = jnp.sum(jnp.where(lane == e, cur_g[...], 0.0), axis=1, keepdims=True)
        yg_ref[0] = y * g

    @pl.when(e % 2 == 0)
    def _():
        compute(xa_ref, ga_ref, xb_ref, gb_ref)

    @pl.when(e % 2 == 1)
    def _():
        compute(xb_ref, gb_ref, xa_ref, ga_ref)


def _experts(idx3, xn, aff, wg, wu, wd, *, batch, seq, cap):
    d = wg.shape[1]
    ff = wg.shape[2]
    ne = N_EXPERTS
    once = pl.Buffered(1)
    return pl.pallas_call(
        functools.partial(_expert_kernel, cap=cap),
        grid=(batch, ne),
        in_specs=[pl.BlockSpec((ne, 1, cap), lambda b, e: (b, 0, 0), memory_space=pltpu.SMEM),
                  pl.BlockSpec((seq, d), lambda b, e: (b, 0), pipeline_mode=once),
                  pl.BlockSpec((seq, LANES), lambda b, e: (b, 0), pipeline_mode=once),
                  pl.BlockSpec((1, d, ff), lambda b, e: (e, 0, 0)),
                  pl.BlockSpec((1, d, ff), lambda b, e: (e, 0, 0)),
                  pl.BlockSpec((1, ff, d), lambda b, e: (e, 0, 0))],
        out_specs=pl.BlockSpec((1, cap, d), lambda b, e: (b * ne + e, 0, 0)),
        out_shape=jax.ShapeDtypeStruct((batch * ne, cap, d), F32),
        scratch_shapes=[pltpu.VMEM((cap, d), F32), pltpu.VMEM((cap, LANES), F32),
                        pltpu.VMEM((cap, d), F32), pltpu.VMEM((cap, LANES), F32)],
        compiler_params=_params(2),
        name="experts",
    )(idx3, xn, aff, wg, wu, wd)


def _combine_kernel(idx_ref, bnd_ref, h1_ref, yg_ref, gfin_ref, out_ref, *, sq):
    q = pl.program_id(1)
    e = pl.program_id(2)

    @pl.when(e == 0)
    def _():
        out_ref[...] = h1_ref[...]

    base = q * sq

    def scatter(i, c):
        t = idx_ref[0, 0, i] - base
        out_ref[pl.ds(t, 1), :] += yg_ref[0, pl.ds(i, 1), :]
        return c

    def scatter4(k, c):
        i = lo + 4 * k
        ts = [idx_ref[0, 0, i + u] - base for u in range(4)]
        new = [out_ref[pl.ds(ts[u], 1), :] + yg_ref[0, pl.ds(i + u, 1), :] for u in range(4)]
        for u in range(4):
            out_ref[pl.ds(ts[u], 1), :] = new[u]
        return c

    lo = bnd_ref[0, q, e]
    hi = bnd_ref[0, q + 1, e]
    quads = lax.shift_right_logical(hi - lo, 2)
    lax.fori_loop(0, quads, scatter4, 0)
    lax.fori_loop(lo + 4 * quads, hi, scatter, 0)

    @pl.when(e == pl.num_programs(2) - 1)
    def _():
        out_ref[...] = _rms(out_ref[...], gfin_ref[...])


def _combine(idx3, bnd, h1, yg, gfin, *, batch, seq, cap, nsplit):
    d = h1.shape[1]
    ne = N_EXPERTS
    sq = seq // nsplit
    return pl.pallas_call(
        functools.partial(_combine_kernel, sq=sq),
        grid=(batch, nsplit, ne),
        in_specs=[pl.BlockSpec((1, 1, cap), lambda b, q, e: (b * ne + e, 0, 0), memory_space=pltpu.SMEM),
                  pl.BlockSpec((1, 8, LANES), lambda b, q, e: (b, 0, 0), memory_space=pltpu.SMEM),
                  pl.BlockSpec((sq, d), lambda b, q, e: (b * nsplit + q, 0)),
                  pl.BlockSpec((1, cap, d), lambda b, q, e: (b * ne + e, 0, 0)),
                  _full(gfin.shape)],
        out_specs=pl.BlockSpec((sq, d), lambda b, q, e: (b * nsplit + q, 0)),
        out_shape=jax.ShapeDtypeStruct((batch * seq, d), F32),
        compiler_params=_params(3),
        name="combine",
    )(idx3, bnd, h1, yg, gfin)


def _rope_tables(seq):
    t = jnp.arange(seq)
    half = MLA_ROPE // 2
    inv = ROPE_BASE ** (-jnp.arange(0, half, 2, dtype=F32) / half)
    ang_r = (t // GRID_W).astype(F32)[:, None] * inv[None, :]
    ang_c = (t % GRID_W).astype(F32)[:, None] * inv[None, :]
    cos = jnp.concatenate([jnp.cos(ang_r), jnp.cos(ang_r), jnp.cos(ang_c), jnp.cos(ang_c)], axis=1)
    sin = jnp.concatenate([-jnp.sin(ang_r), jnp.sin(ang_r), -jnp.sin(ang_c), jnp.sin(ang_c)], axis=1)
    pad = LANES - MLA_NOPE - MLA_ROPE
    cos = jnp.concatenate([jnp.ones((seq, MLA_NOPE), F32), cos, jnp.zeros((seq, pad), F32)], axis=1)
    sin = jnp.concatenate([jnp.zeros((seq, MLA_NOPE), F32), sin, jnp.zeros((seq, pad), F32)], axis=1)
    return cos, sin


_q = MLA_ROPE // 4
ROPE_SWAP = np.concatenate([np.arange(_q, 2 * _q), np.arange(0, _q),
                            np.arange(3 * _q, 4 * _q), np.arange(2 * _q, 3 * _q)])


def kernel(x, mem, g_mix, w_in, na_rpb, mla_q_norm, w_mla_uq, mla_kv_norm, w_mla_ukv, g_mem, w_mem_kv,
           w_o_na, w_o_mla, w_o_mem, w_out, g_ffn, w_router, b_router, w_gate, w_up, w_down, g_final):
    batch, seq, d = x.shape
    n_mem = mem.shape[1]
    depth = g_mix.shape[0]
    q_rank = mla_q_norm.shape[1]
    kv_rank = mla_kv_norm.shape[1]
    cap = EC_FACTOR * seq // N_EXPERTS
    nsplit = 2
    tm = 512
    pad = LANES - MLA_NOPE - MLA_ROPE
    cos, sin = _rope_tables(seq)

    assert depth == 1
    h = x.reshape(batch * seq, d)
    for l in range(depth):
        w = w_in[l]
        o = np.cumsum([0, NA_W, NA_W, NA_W, q_rank, kv_rank, MLA_ROPE, MEM_W, 3 * d])
        wna = w[:, o[0]:o[3]].astype(BF16)
        wcq = w[:, o[3]:o[4]].astype(BF16)
        wckv = w[:, o[4]:o[5]].astype(BF16)
        wkr_raw = w[:, o[5]:o[6]]
        wqm = w[:, o[6]:o[7]].astype(BF16)
        wgate_cols = w[:, o[7]:o[8]].astype(BF16)
        z = lambda *s: jnp.zeros(s, F32)
        wkr = jnp.concatenate([z(d, MLA_NOPE), wkr_raw, z(d, pad),
                               z(d, MLA_NOPE), wkr_raw[:, ROPE_SWAP], z(d, pad)], axis=1).astype(BF16)
        uq = w_mla_uq[l].reshape(q_rank, MLA_HEADS, MLA_NOPE + MLA_ROPE)
        uq_a = jnp.concatenate([uq, z(q_rank, MLA_HEADS, pad)], axis=2)
        uq_b = jnp.concatenate([z(q_rank, MLA_HEADS, MLA_NOPE), uq[:, :, MLA_NOPE:][:, :, ROPE_SWAP],
                                z(q_rank, MLA_HEADS, pad)], axis=2)
        wuq = jnp.concatenate([uq_a.reshape(q_rank, -1), uq_b.reshape(q_rank, -1)], axis=1).astype(BF16)
        ukv = w_mla_ukv[l].reshape(kv_rank, MLA_HEADS, MLA_NOPE + MLA_V)
        uk = jnp.concatenate([ukv[:, :, :MLA_NOPE], z(kv_rank, MLA_HEADS, LANES - MLA_NOPE)], axis=2)
        uv = jnp.concatenate([ukv[:, :, MLA_NOPE:], z(kv_rank, MLA_HEADS, LANES - MLA_V)], axis=2)
        wukv = jnp.concatenate([uk.reshape(kv_rank, -1), uv.reshape(kv_rank, -1)], axis=1).astype(BF16)
        wr = jnp.concatenate([w_router[l].astype(F32), z(d, LANES - N_EXPERTS)], axis=1)
        wr_hi = lax.bitcast_convert_type(lax.bitcast_convert_type(wr, jnp.int32) & jnp.int32(-65536), F32)
        wrh = wr_hi.astype(BF16)
        wrl = (wr - wr_hi).astype(BF16)
        br = jnp.concatenate([b_router[l].astype(F32), jnp.full((LANES - N_EXPERTS,), NEG, F32)])[None, :]

        na, qmla, kmla, vmla, qmem = _inproj(
            h, g_mix[l][None, :], wna, wcq, wckv, wkr, wqm, mla_q_norm[l][None, :], mla_kv_norm[l][None, :],
            wuq, wukv, cos, sin, seq=seq, tm=tm)
        y_na = _na_attention(na, _na_bias_table(na_rpb[l]), batch=batch, seq=seq)
        y_mla = _mla_attention(qmla, kmla, vmla, batch=batch, seq=seq, tq=256)
        kvm = _memkv(mem.reshape(batch * n_mem, d), g_mem[l][None, :], w_mem_kv[l].astype(BF16),
                     batch=batch, n_mem=n_mem)
        h1, xn, lg = _merge(h, y_na, y_mla, qmem, kvm, g_mix[l][None, :], wgate_cols,
                             w_o_na[l].astype(BF16), w_o_mla[l].astype(BF16), w_o_mem[l].astype(BF16),
                             w_out[l].astype(BF16), g_ffn[l][None, :], wrh, wrl, br,
                             seq=seq, n_mem=n_mem, tm=512)
        idx, aff, bnd = _route(lg, batch=batch, seq=seq, cap=cap, nsplit=nsplit)
        idx3 = idx.reshape(batch * N_EXPERTS, 1, cap)
        yg = _experts(idx3, xn, aff, w_gate[l].astype(BF16), w_up[l].astype(BF16), w_down[l].astype(BF16),
                      batch=batch, seq=seq, cap=cap)
        h = _combine(idx3, bnd, h1, yg, g_final[None, :], batch=batch, seq=seq, cap=cap, nsplit=nsplit)
    return h.reshape(batch, seq, d)
```
